```python
import jax
import jax.numpy as jnp
from jax import lax
import numpy as np

D_MODEL = 2048
BATCH = 4
SEQ = 2048
DEPTH = 2
DEC_BATCH = 16
DEC_SEQ = 16
PAST_LEN = 4096

CHUNK = 64
N_MEM = 256
GLA_H = 4
GLA_DK = 128
GLA_DV = 256
GATE_RANK = 16
GATE_TAU = 16.0
DSA_H = 8
DSA_KVH = 2
DSA_HD = 128
IDX_H = 16
IDX_D = 64
TOPK_MAX = 256
Q_BLOCK = 128
XA_H = 4
XA_HD = D_MODEL // XA_H
N_EXPERTS = 16
N_GROUPS = 4
EXP_PER_GROUP = N_EXPERTS // N_GROUPS
MOE_TOPK = 2
D_FF = 512
ROPE_THETA = 10000.0
LN_EPS = 1e-5
DN_ALPHA = (2 * DEPTH) ** 0.25
DN_BETA = (8 * DEPTH) ** -0.25

MIX = GLA_H * GLA_DV + DSA_H * DSA_HD
IN_SPLITS = (GLA_H * GLA_DK, GLA_H * GLA_DK, GLA_H * GLA_DV, GATE_RANK, GLA_H * GLA_DV,
             DSA_H * DSA_HD, DSA_KVH * DSA_HD, DSA_KVH * DSA_HD, IDX_H * IDX_D, IDX_D, IDX_H)
P_IN = sum(IN_SPLITS)
SPLIT_POINTS = tuple(int(c) for c in np.cumsum(IN_SPLITS)[:-1])

kernel_name = 'hybrid_gla_dsa_streaming_encoder_step'


def layer_norm(x, g, b):
    xf = x.astype(jnp.float32)
    mu = xf.mean(-1, keepdims=True)
    var = jnp.square(xf - mu).mean(-1, keepdims=True)
    return ((xf - mu) * lax.rsqrt(var + LN_EPS) * g + b).astype(x.dtype)


def rope(x, pos):
    half = x.shape[-1] // 2
    inv_freq = ROPE_THETA ** (-jnp.arange(half, dtype=jnp.float32) / half)
    ang = pos.astype(jnp.float32)[:, None] * inv_freq[None, :]
    ang = ang.reshape((ang.shape[0],) + (1,) * (x.ndim - 3) + (half,))
    cos, sin = jnp.cos(ang), jnp.sin(ang)
    xf = x.astype(jnp.float32)
    x1, x2 = xf[..., :half], xf[..., half:]
    return jnp.concatenate([x1 * cos - x2 * sin, x2 * cos + x1 * sin], -1).astype(x.dtype)


def gla_prepare(q, k, v, lr, wa2, ba):
    B, T = q.shape[:2]
    q = q.reshape(B, T, GLA_H, GLA_DK) * (GLA_DK ** -0.5)
    k = k.reshape(B, T, GLA_H, GLA_DK)
    v = v.reshape(B, T, GLA_H, GLA_DV)
    g = jax.nn.log_sigmoid((lr @ wa2 + ba).astype(jnp.float32)) / GATE_TAU
    return q, k, v, g.reshape(B, T, GLA_H, GLA_DK)


def gla_chunk(S, q, k, v, g):
    T = q.shape[1]
    G = jnp.cumsum(g, axis=1)
    o_inter = jnp.einsum('bthk,bhkv->bthv', q * jnp.exp(G), S)
    causal = jnp.tril(jnp.ones((T, T), dtype=bool))[None, :, :, None, None]
    decay = jnp.exp(jnp.where(causal, G[:, :, None] - G[:, None, :], -jnp.inf))
    A = jnp.einsum('bthk,bshk,btshk->bhts', q, k, decay)
    o_intra = jnp.einsum('bhts,bshv->bthv', A, v)
    G_last = G[:, -1]
    k_dec = k * jnp.exp(G_last[:, None] - G)
    S_new = jnp.exp(G_last)[..., None] * S + jnp.einsum('bshk,bshv->bhkv', k_dec, v)
    return S_new, o_inter + o_intra


def gla_output(o, og, gnorm):
    B, T = o.shape[:2]
    of = o.astype(jnp.float32)
    n = of * lax.rsqrt(jnp.mean(of * of, -1, keepdims=True) + LN_EPS) * gnorm
    return n.reshape(B, T, GLA_H * GLA_DV) * jax.nn.silu(og.astype(jnp.float32))


def dsa_prepare(q, k, v, qi, ki, wi, pos):
    B, T = q.shape[:2]
    q = rope(q.reshape(B, T, DSA_H, DSA_HD), pos)
    k = rope(k.reshape(B, T, DSA_KVH, DSA_HD), pos)
    v = v.reshape(B, T, DSA_KVH, DSA_HD)
    qi = rope(qi.reshape(B, T, IDX_H, IDX_D), pos)
    ki = rope(ki, pos)
    wi = wi * (IDX_H ** -0.5)
    return q, k, v, qi, ki, wi


def dsa_attend(q, qi, wi, k_all, v_all, ki_all, admissible, topk):
    B, Q = q.shape[:2]
    logits = jnp.einsum('bqhd,bld->bqhl', qi, ki_all).astype(jnp.float32) * (IDX_D ** -0.5)
    score = jnp.einsum('bqh,bqhl->bql', wi.astype(jnp.float32), jax.nn.relu(logits))
    if admissible is not None:
        score = jnp.where(admissible[None], score, -jnp.inf)
    top_val, sel = lax.top_k(score, topk)
    gather = jax.vmap(lambda rows, idx: rows[idx])
    k_sel = gather(k_all, sel)
    v_sel = gather(v_all, sel)
    qg = q.reshape(B, Q, DSA_KVH, DSA_H // DSA_KVH, DSA_HD)
    s = jnp.einsum('bqjgd,bqnjd->bqjgn', qg, k_sel).astype(jnp.float32) * (DSA_HD ** -0.5)
    s = jnp.where(jnp.isfinite(top_val)[:, :, None, None, :], s, -jnp.inf)
    p = jax.nn.softmax(s, axis=-1).astype(v_sel.dtype)
    o = jnp.einsum('bqjgn,bqnjd->bqjgd', p, v_sel)
    return o.reshape(B, Q, DSA_H * DSA_HD)


def mem_kv(mem, wk, wv):
    B, M = mem.shape[:2]
    return (mem @ wk).reshape(B, M, XA_H, XA_HD), (mem @ wv).reshape(B, M, XA_H, XA_HD)


def cross_attend(x, mk, mv, wq, wo):
    B, T = x.shape[:2]
    q = (x @ wq).reshape(B, T, XA_H, XA_HD)
    s = jnp.einsum('bthd,bmhd->bhtm', q, mk).astype(jnp.float32) * (XA_HD ** -0.5)
    p = jax.nn.softmax(s, axis=-1).astype(mv.dtype)
    o = jnp.einsum('bhtm,bmhd->bthd', p, mv).reshape(B, T, XA_H * XA_HD)
    return o @ wo


def moe(x, router_w, router_b, w_gate, w_up, w_down):
    B, T, D = x.shape
    h = x.reshape(B * T, D)
    aff = jax.nn.sigmoid((h @ router_w).astype(jnp.float32))
    biased = aff + router_b
    grp_score = lax.top_k(biased.reshape(-1, N_GROUPS, EXP_PER_GROUP), MOE_TOPK)[0].sum(-1)
    grp = jnp.argmax(grp_score, axis=-1)
    in_grp = (jnp.arange(N_EXPERTS) // EXP_PER_GROUP)[None, :] == grp[:, None]
    _, eidx = lax.top_k(jnp.where(in_grp, biased, -jnp.inf), MOE_TOPK)
    gw = jnp.take_along_axis(aff, eidx, axis=-1)
    gw = gw / gw.sum(-1, keepdims=True)
    combine = jnp.einsum('nk,nke->ne', gw, jax.nn.one_hot(eidx, N_EXPERTS, dtype=jnp.float32))
    hid = jax.nn.silu(jnp.einsum('nd,edf->nef', h, w_gate)) * jnp.einsum('nd,edf->nef', h, w_up)
    hid = hid * combine.astype(hid.dtype)[:, :, None]
    return jnp.einsum('nef,efd->nd', hid, w_down).reshape(B, T, D)


def finish_layer(x, mix, mk, mv, lp, router_w, router_b):
    (_, _, _, _, _, ln1_g, ln1_b, xa_wq, _, _, xa_wo, ln2_g, ln2_b,
     w_gate, w_up, w_down, ln3_g, ln3_b) = lp
    x = layer_norm(DN_ALPHA * x + mix.astype(x.dtype), ln1_g, ln1_b)
    x = layer_norm(DN_ALPHA * x + cross_attend(x, mk, mv, xa_wq, xa_wo), ln2_g, ln2_b)
    x = layer_norm(DN_ALPHA * x + moe(x, router_w, router_b, w_gate, w_up, w_down).astype(x.dtype), ln3_g, ln3_b)
    return x


def prompt_layer(x, mem, lp, router_w, router_b):
    (w_in, gla_wa2, gla_ba, gla_norm_g, w_out, _, _, _, xa_wk, xa_wv, _, _, _,
     _, _, _, _, _) = lp
    B, T, _ = x.shape
    gq, gk, gv, glr, gog, dq, dk, dv, iq, ik, iw = jnp.split(x @ w_in, SPLIT_POINTS, axis=-1)
    pos = jnp.arange(T)
    q, k, v, g = gla_prepare(gq, gk, gv, glr, gla_wa2, gla_ba)
    nc = T // CHUNK
    def chunks(a):
        return a.reshape((B, nc, CHUNK) + a.shape[2:]).swapaxes(0, 1)
    S0 = jnp.zeros((B, GLA_H, GLA_DK, GLA_DV), jnp.float32)
    S_fin, o = lax.scan(lambda S, c: gla_chunk(S, *c), S0, (chunks(q), chunks(k), chunks(v), chunks(g)))
    o_gla = o.swapaxes(0, 1).reshape(B, T, GLA_H, GLA_DV)
    q_d, k_d, v_d, qi, ki, wi = dsa_prepare(dq, dk, dv, iq, ik, iw, pos)
    topk = min(TOPK_MAX, T // 4)
    nb = T // Q_BLOCK
    def qblocks(a):
        return a.reshape((B, nb, Q_BLOCK) + a.shape[2:]).swapaxes(0, 1)
    key_chunk = jnp.arange(T) // CHUNK
    q_chunk = key_chunk.reshape(nb, Q_BLOCK)
    def attend_block(c):
        qb, qib, wib, qcb = c
        adm = key_chunk[None, :] <= qcb[:, None]
        return dsa_attend(qb, qib, wib, k_d, v_d, ki, adm, topk)
    o_d = lax.map(attend_block, (qblocks(q_d), qblocks(qi), qblocks(wi), q_chunk))
    o_dsa = o_d.swapaxes(0, 1).reshape(B, T, DSA_H * DSA_HD)
    mix = jnp.concatenate([gla_output(o_gla, gog, gla_norm_g).astype(x.dtype), o_dsa.astype(x.dtype)], -1) @ w_out
    mk, mv = mem_kv(mem, xa_wk, xa_wv)
    y = finish_layer(x, mix, mk, mv, lp, router_w, router_b)
    return y, (k_d, v_d, ki, S_fin, mk, mv)


def sample_layer(x, c_k, c_v, c_ik, S0, mk, mv, lp, router_w, router_b):
    (w_in, gla_wa2, gla_ba, gla_norm_g, w_out, _, _, _, _, _, _, _, _,
     _, _, _, _, _) = lp
    B, T, _ = x.shape
    past = c_k.shape[1]
    gq, gk, gv, glr, gog, dq, dk, dv, iq, ik, iw = jnp.split(x @ w_in, SPLIT_POINTS, axis=-1)
    pos = past + jnp.arange(T)
    q, k, v, g = gla_prepare(gq, gk, gv, glr, gla_wa2, gla_ba)
    S_new, o_gla = gla_chunk(S0, q, k, v, g)
    q_d, k_d, v_d, qi, ki, wi = dsa_prepare(dq, dk, dv, iq, ik, iw, pos)
    k_all = jnp.concatenate([c_k, k_d.astype(c_k.dtype)], axis=1)
    v_all = jnp.concatenate([c_v, v_d.astype(c_v.dtype)], axis=1)
    ki_all = jnp.concatenate([c_ik, ki.astype(c_ik.dtype)], axis=1)
    topk = min(TOPK_MAX, (past + T) // 4)
    o_dsa = dsa_attend(q_d, qi, wi, k_all, v_all, ki_all, None, topk)
    mix = jnp.concatenate([gla_output(o_gla, gog, gla_norm_g).astype(x.dtype), o_dsa.astype(x.dtype)], -1) @ w_out
    y = finish_layer(x, mix, mk, mv, lp, router_w, router_b)
    return y, (k_d, v_d, ki, S_new)


def setup_inputs(seed: int = 0) -> dict:
    key = jax.random.key(seed)
    ks = jax.random.split(key, 31)
    f32 = jnp.float32
    def nrm(i, shape, scale=1.0):
        return jax.random.normal(ks[i], shape, f32) * scale
    def gain(i, shape):
        return 1.0 + nrm(i, shape, 0.05)
    return {
        'x_prompt': nrm(0, (BATCH, SEQ, D_MODEL)),
        'x_sample': nrm(1, (DEC_BATCH, DEC_SEQ, D_MODEL)),
        'cache_dsa_k': nrm(2, (DEPTH, DEC_BATCH, PAST_LEN, DSA_KVH, DSA_HD)),
        'cache_dsa_v': nrm(3, (DEPTH, DEC_BATCH, PAST_LEN, DSA_KVH, DSA_HD)),
        'cache_idx_k': nrm(4, (DEPTH, DEC_BATCH, PAST_LEN, IDX_D)),
        'state_gla': nrm(5, (DEPTH, DEC_BATCH, GLA_H, GLA_DK, GLA_DV), 0.5),
        'cache_mem_k': nrm(6, (DEPTH, DEC_BATCH, N_MEM, XA_H, XA_HD)),
        'cache_mem_v': nrm(7, (DEPTH, DEC_BATCH, N_MEM, XA_H, XA_HD)),
        'mem_prompt': nrm(8, (BATCH, N_MEM, D_MODEL)),
        'ln_in_g': gain(9, (D_MODEL,)),
        'ln_in_b': nrm(10, (D_MODEL,), 0.05),
        'w_in': nrm(11, (DEPTH, D_MODEL, P_IN), D_MODEL ** -0.5),
        'gla_wa2': nrm(12, (DEPTH, GATE_RANK, GLA_H * GLA_DK), GATE_RANK ** -0.5),
        'gla_ba': nrm(13, (DEPTH, GLA_H * GLA_DK), 0.1),
        'gla_norm_g': gain(14, (DEPTH, GLA_DV)),
        'w_out': nrm(15, (DEPTH, MIX, D_MODEL), DN_BETA * MIX ** -0.5),
        'ln1_g': gain(16, (DEPTH, D_MODEL)),
        'ln1_b': nrm(17, (DEPTH, D_MODEL), 0.05),
        'xa_wq': nrm(18, (DEPTH, D_MODEL, XA_H * XA_HD), D_MODEL ** -0.5),
        'xa_wk': nrm(19, (DEPTH, D_MODEL, XA_H * XA_HD), D_MODEL ** -0.5),
        'xa_wv': nrm(20, (DEPTH, D_MODEL, XA_H * XA_HD), D_MODEL ** -0.5),
        'xa_wo': nrm(21, (DEPTH, XA_H * XA_HD, D_MODEL), DN_BETA * D_MODEL ** -0.5),
        'ln2_g': gain(22, (DEPTH, D_MODEL)),
        'ln2_b': nrm(23, (DEPTH, D_MODEL), 0.05),
        'router_w': nrm(24, (D_MODEL, N_EXPERTS), D_MODEL ** -0.5),
        'router_b': nrm(25, (N_EXPERTS,), 0.01),
        'moe_w_gate': nrm(26, (DEPTH, N_EXPERTS, D_MODEL, D_FF), D_MODEL ** -0.5),
        'moe_w_up': nrm(27, (DEPTH, N_EXPERTS, D_MODEL, D_FF), D_MODEL ** -0.5),
        'moe_w_down': nrm(28, (DEPTH, N_EXPERTS, D_FF, D_MODEL), DN_BETA * D_FF ** -0.5),
        'ln3_g': gain(29, (DEPTH, D_MODEL)),
        'ln3_b': nrm(30, (DEPTH, D_MODEL), 0.05),
    }


def reference(x_prompt, x_sample, cache_dsa_k, cache_dsa_v, cache_idx_k, state_gla, cache_mem_k, cache_mem_v,
              mem_prompt, ln_in_g, ln_in_b, w_in, gla_wa2, gla_ba, gla_norm_g, w_out, ln1_g, ln1_b,
              xa_wq, xa_wk, xa_wv, xa_wo, ln2_g, ln2_b, router_w, router_b,
              moe_w_gate, moe_w_up, moe_w_down, ln3_g, ln3_b):
    xp = layer_norm(x_prompt, ln_in_g, ln_in_b)
    xs = layer_norm(x_sample, ln_in_g, ln_in_b)
    p_states = []
    s_states = []
    for l in range(DEPTH):
        lp = (w_in[l], gla_wa2[l], gla_ba[l], gla_norm_g[l], w_out[l], ln1_g[l], ln1_b[l],
              xa_wq[l], xa_wk[l], xa_wv[l], xa_wo[l], ln2_g[l], ln2_b[l],
              moe_w_gate[l], moe_w_up[l], moe_w_down[l], ln3_g[l], ln3_b[l])
        xp, ps = prompt_layer(xp, mem_prompt, lp, router_w, router_b)
        xs, ss = sample_layer(xs, cache_dsa_k[l], cache_dsa_v[l], cache_idx_k[l], state_gla[l],
                              cache_mem_k[l], cache_mem_v[l], lp, router_w, router_b)
        p_states.append(ps)
        s_states.append(ss)
    def stack(states, i):
        return jnp.stack([st[i] for st in states])
    return (xp, xs,
            stack(p_states, 0), stack(p_states, 1), stack(p_states, 2), stack(p_states, 3),
            stack(p_states, 4), stack(p_states, 5),
            stack(s_states, 0), stack(s_states, 1), stack(s_states, 2), stack(s_states, 3))
```

```python
import functools

import jax
import jax.numpy as jnp
import numpy as np
from jax import lax
from jax.experimental import pallas as pl
from jax.experimental.pallas import tpu as pltpu

F32 = jnp.float32
BF16 = jnp.bfloat16

D_MODEL = 2048
BATCH = 4
SEQ = 2048
DEPTH = 2
DEC_BATCH = 16
DEC_SEQ = 16
PAST_LEN = 4096
CHUNK = 64
N_MEM = 256
GLA_H = 4
GLA_DK = 128
GLA_DV = 256
GATE_RANK = 16
GATE_TAU = 16.0
DSA_H = 8
DSA_KVH = 2
DSA_HD = 128
IDX_H = 16
IDX_D = 64
TOPK_MAX = 256
XA_H = 4
XA_HD = D_MODEL // XA_H
N_EXPERTS = 16
N_GROUPS = 4
EXP_PER_GROUP = N_EXPERTS // N_GROUPS
D_FF = 512
ROPE_THETA = 10000.0
LN_EPS = 1e-5
DN_ALPHA = (2 * DEPTH) ** 0.25

IN_SPLITS = (GLA_H * GLA_DK, GLA_H * GLA_DK, GLA_H * GLA_DV, GATE_RANK, GLA_H * GLA_DV,
             DSA_H * DSA_HD, DSA_KVH * DSA_HD, DSA_KVH * DSA_HD, IDX_H * IDX_D, IDX_D, IDX_H)
SPLIT_POINTS = tuple(int(c) for c in np.cumsum(IN_SPLITS)[:-1])

N_PROMPT = BATCH * SEQ
N_SAMPLE = DEC_BATCH * DEC_SEQ
N_TOK = N_PROMPT + N_SAMPLE

LANE = 128
GLA_SUB = 16
VMEM_LIMIT = 48 * 1024 * 1024

G_Q, G_K, G_V, G_OG, G_DV, G_MISC = 0, 512, 1024, 2048, 3072, 3328
WG_COLS = 3456
WR_COLS = 1280
WI_COLS = 1152
MISC_LR = 0
MISC_IW = GATE_RANK


def _cparams(sem):
    return pltpu.CompilerParams(dimension_semantics=sem, vmem_limit_bytes=VMEM_LIMIT)


def _dot(a, b):
    return jnp.dot(a, b, preferred_element_type=F32)


def _dot_nt(a, b):
    return lax.dot_general(a, b, (((1,), (1,)), ((), ())), preferred_element_type=F32)


def _dot_tn(a, b):
    return lax.dot_general(a, b, (((0,), (0,)), ((), ())), preferred_element_type=F32)


def _layer_norm_rows(y, g, b):
    mu = jnp.mean(y, axis=-1, keepdims=True)
    d = y - mu
    var = jnp.mean(d * d, axis=-1, keepdims=True)
    return d * lax.rsqrt(var + LN_EPS) * g + b


def _ln_kernel(x_ref, g_ref, b_ref, o_ref, ob_ref):
    y = _layer_norm_rows(x_ref[...], g_ref[...], b_ref[...])
    o_ref[...] = y
    ob_ref[...] = y.astype(BF16)


def layer_norm_rows(x, g, b, tm):
    n, d = x.shape
    return pl.pallas_call(
        _ln_kernel,
        grid=(n // tm,),
        in_specs=[pl.BlockSpec((tm, d), lambda i: (i, 0)),
                  pl.BlockSpec((1, d), lambda i: (0, 0)),
                  pl.BlockSpec((1, d), lambda i: (0, 0))],
        out_specs=[pl.BlockSpec((tm, d), lambda i: (i, 0)),
                   pl.BlockSpec((tm, d), lambda i: (i, 0))],
        out_shape=[jax.ShapeDtypeStruct((n, d), F32), jax.ShapeDtypeStruct((n, d), BF16)],
        compiler_params=_cparams(("parallel",)),
        name="ln_in",
    )(x, g.reshape(1, d), b.reshape(1, d))


def _swap_halves(x, half):
    if half == LANE // 2:
        return pltpu.roll(x, half, 1)
    lane = lax.broadcasted_iota(jnp.int32, x.shape, 1)
    lower = (lane % (2 * half)) < half
    return jnp.where(lower, pltpu.roll(x, LANE - half, 1), pltpu.roll(x, half, 1))


def _mm_kernel(a_ref, w_ref, *rest, rope_half, tn):
    acc = _dot(a_ref[...], w_ref[...])
    if rope_half is None:
        (o_ref,) = rest
        o_ref[...] = acc.astype(o_ref.dtype)
        return
    cos_ref, sin_ref, o_ref = rest
    cos = cos_ref[...]
    sin = sin_ref[...]
    for c in range(tn // LANE):
        x = acc[:, c * LANE:(c + 1) * LANE]
        o_ref[:, c * LANE:(c + 1) * LANE] = (x * cos + _swap_halves(x, rope_half) * sin).astype(o_ref.dtype)


def matmul(a, w, *, tm, tn, out_dtype=F32, rope=None, name="mm"):
    m, k = a.shape
    n = w.shape[1]
    in_specs = [pl.BlockSpec((tm, k), lambda j, i: (i, 0)),
                pl.BlockSpec((k, tn), lambda j, i: (0, j))]
    args = [a, w]
    rope_half = None
    if rope is not None:
        cos, sin, rope_half = rope
        in_specs += [pl.BlockSpec((tm, LANE), lambda j, i: (i, 0)),
                     pl.BlockSpec((tm, LANE), lambda j, i: (i, 0))]
        args += [cos, sin]
    return pl.pallas_call(
        functools.partial(_mm_kernel, rope_half=rope_half, tn=tn),
        grid=(n // tn, m // tm),
        in_specs=in_specs,
        out_specs=pl.BlockSpec((tm, tn), lambda j, i: (i, j)),
        out_shape=jax.ShapeDtypeStruct((m, n), out_dtype),
        compiler_params=_cparams(("parallel", "parallel")),
        name=name,
    )(*args)


def _mm_ln_kernel(*refs, n_pairs):
    a_refs = refs[:n_pairs]
    w_refs = refs[n_pairs:2 * n_pairs]
    resid_ref, g_ref, b_ref, o_ref, ob_ref = refs[2 * n_pairs:]
    acc = _dot(a_refs[0][...], w_refs[0][...])
    for a_ref, w_ref in zip(a_refs[1:], w_refs[1:]):
        acc = acc + _dot(a_ref[...], w_ref[...])
    y = _layer_norm_rows(DN_ALPHA * resid_ref[...] + acc, g_ref[...], b_ref[...])
    o_ref[...] = y
    ob_ref[...] = y.astype(BF16)


def matmul_ln(a_list, w_list, resid, g, b, *, tm, name):
    m, d = resid.shape
    n_pairs = len(a_list)
    in_specs = ([pl.BlockSpec((tm, a.shape[1]), lambda i: (i, 0)) for a in a_list]
                + [pl.BlockSpec(w.shape, lambda i: (0, 0)) for w in w_list]
                + [pl.BlockSpec((tm, d), lambda i: (i, 0)),
                   pl.BlockSpec((1, d), lambda i: (0, 0)),
                   pl.BlockSpec((1, d), lambda i: (0, 0))])
    return pl.pallas_call(
        functools.partial(_mm_ln_kernel, n_pairs=n_pairs),
        grid=(m // tm,),
        in_specs=in_specs,
        out_specs=[pl.BlockSpec((tm, d), lambda i: (i, 0)),
                   pl.BlockSpec((tm, d), lambda i: (i, 0))],
        out_shape=[jax.ShapeDtypeStruct((m, d), F32), jax.ShapeDtypeStruct((m, d), BF16)],
        compiler_params=_cparams(("parallel",)),
        name=name,
    )(*a_list, *w_list, resid, g.reshape(1, d), b.reshape(1, d))


def _log_sigmoid(z):
    return jnp.minimum(z, 0.0) - jnp.log1p(jnp.exp(-jnp.abs(z)))


def _gla_kernel(*refs, tb, chunk, sub, has_state):
    if has_state:
        q_ref, k_ref, v_ref, og_ref, misc_ref, wa2_ref, ba_ref, gn_ref, s0_ref, o_ref, sout_ref, st_ref = refs
    else:
        q_ref, k_ref, v_ref, og_ref, misc_ref, wa2_ref, ba_ref, gn_ref, o_ref, sout_ref, st_ref = refs
    t = pl.program_id(2)

    @pl.when(t == 0)
    def _():
        if has_state:
            st_ref[...] = s0_ref[...].T
        else:
            st_ref[...] = jnp.zeros_like(st_ref)

    z = jnp.dot(misc_ref[...], wa2_ref[...], preferred_element_type=F32,
                precision=lax.Precision.HIGHEST) + ba_ref[...]
    g_all = _log_sigmoid(z) * (1.0 / GATE_TAU)

    row = lax.broadcasted_iota(jnp.int32, (chunk, chunk), 0)
    col = lax.broadcasted_iota(jnp.int32, (chunk, chunk), 1)
    tril = (row >= col).astype(F32)
    gn = gn_ref[...]

    for c in range(tb // chunk):
        rows = slice(c * chunk, (c + 1) * chunk)
        G = jnp.dot(tril, g_all[rows], preferred_element_type=F32,
                    precision=lax.Precision.HIGHEST)
        qc = q_ref[rows, :] * (GLA_DK ** -0.5)
        kc = k_ref[rows, :]
        vb = v_ref[rows, :].astype(BF16)
        st = st_ref[...]
        o = _dot_nt((qc * jnp.exp(G)).astype(BF16), st.astype(BF16))
        o_parts = []
        for i in range(chunk // sub):
            r0 = i * sub
            nk = r0 + sub
            b_i = G[r0:r0 + 1, :]
            qt = qc[r0:nk] * jnp.exp(G[r0:nk] - b_i)
            kt = kc[:nk] * jnp.exp(b_i - G[:nk])
            a = _dot_nt(qt.astype(BF16), kt.astype(BF16))
            t_idx = r0 + lax.broadcasted_iota(jnp.int32, (sub, nk), 0)
            s_idx = lax.broadcasted_iota(jnp.int32, (sub, nk), 1)
            a = jnp.where(t_idx >= s_idx, a, 0.0)
            o_parts.append(_dot(a.astype(BF16), vb[:nk]))
        o = o + jnp.concatenate(o_parts, axis=0)
        g_last = G[chunk - 1:chunk, :]
        k_dec = kc * jnp.exp(g_last - G)
        st_ref[...] = st * jnp.exp(g_last) + _dot_tn(vb, k_dec.astype(BF16))
        n = o * lax.rsqrt(jnp.mean(o * o, axis=-1, keepdims=True) + LN_EPS) * gn
        og = og_ref[rows, :]
        o_ref[rows, :] = (n * (og * jax.nn.sigmoid(og))).astype(o_ref.dtype)

    @pl.when(t == pl.num_programs(2) - 1)
    def _():
        sout_ref[...] = st_ref[...].T


def gla_mix(pg, wa2p, ba, gn, s0, *, nb, seq, row0, tb, chunk, sub):
    nt = seq // tb
    rb0 = row0 // tb

    def tok(colblock):
        return lambda b, h, t: (rb0 + b * nt + t, colblock(h))

    in_specs = [
        pl.BlockSpec((tb, GLA_DK), tok(lambda h: G_Q // GLA_DK + h)),
        pl.BlockSpec((tb, GLA_DK), tok(lambda h: G_K // GLA_DK + h)),
        pl.BlockSpec((tb, GLA_DV), tok(lambda h: G_V // GLA_DV + h)),
        pl.BlockSpec((tb, GLA_DV), tok(lambda h: G_OG // GLA_DV + h)),
        pl.BlockSpec((tb, LANE), tok(lambda h: G_MISC // LANE)),
        pl.BlockSpec((None, LANE, GLA_DK), lambda b, h, t: (h, 0, 0)),
        pl.BlockSpec((None, 1, GLA_DK), lambda b, h, t: (h, 0, 0)),
        pl.BlockSpec((1, GLA_DV), lambda b, h, t: (0, 0)),
    ]
    args = [pg, pg, pg, pg, pg, wa2p, ba, gn]
    if s0 is not None:
        in_specs.append(pl.BlockSpec((None, None, GLA_DK, GLA_DV), lambda b, h, t: (b, h, 0, 0)))
        args.append(s0)
    return pl.pallas_call(
        functools.partial(_gla_kernel, tb=tb, chunk=chunk, sub=sub, has_state=s0 is not None),
        grid=(nb, GLA_H, nt),
        in_specs=in_specs,
        out_specs=[pl.BlockSpec((tb, GLA_DV), lambda b, h, t: (b * nt + t, h)),
                   pl.BlockSpec((None, None, GLA_DK, GLA_DV), lambda b, h, t: (b, h, 0, 0))],
        out_shape=[jax.ShapeDtypeStruct((nb * seq, GLA_H * GLA_DV), BF16),
                   jax.ShapeDtypeStruct((nb, GLA_H, GLA_DK, GLA_DV), F32)],
        scratch_shapes=[pltpu.VMEM((GLA_DV, GLA_DK), F32)],
        compiler_params=_cparams(("parallel", "parallel", "arbitrary")),
        name="gla",
    )(*args)


INT_MIN = -2 ** 31


def _order_key(score):
    bits = pltpu.bitcast(score, jnp.int32)
    return jnp.where(bits < 0, bits ^ jnp.int32(0x7FFFFFFF), bits)


def _dsa_kernel(q_ref, qi_ref, misc_ref, k_ref, v_ref, ki_ref, o_ref, key_ref,
                *, nq, n_keys, causal_chunk, n_valid, topk, idx_bits):
    qb = pl.program_id(1)
    lane = lax.broadcasted_iota(jnp.int32, (nq, LANE), 1)
    upper = lane >= IDX_D
    kib = ki_ref[...].astype(BF16)
    misc = misc_ref[...]

    score = jnp.zeros((nq, n_keys), F32)
    for h in range(IDX_H):
        pair = qi_ref[:, (h // 2) * LANE:(h // 2 + 1) * LANE]
        qm = jnp.where(upper == (h % 2 == 1), pair, 0.0).astype(BF16)
        w = misc[:, MISC_IW + h:MISC_IW + h + 1] * (IDX_H ** -0.5 * IDX_D ** -0.5)
        score = score + jnp.maximum(_dot_nt(qm, kib), 0.0) * w

    kidx = lax.broadcasted_iota(jnp.int32, (nq, n_keys), 1)
    if causal_chunk is not None:
        qpos = qb * nq + lax.broadcasted_iota(jnp.int32, (nq, 1), 0)
        limit = (lax.shift_right_logical(qpos, int(np.log2(causal_chunk))) + 1) * causal_chunk
        adm = kidx < limit
    else:
        adm = kidx < n_valid
    neg_inf = jnp.float32(-jnp.inf)
    score = jnp.where(adm, score, neg_inf)
    key_ref[...] = _order_key(score)

    def thr_body(i, t):
        cand = t + lax.shift_left(jnp.int32(1), 31 - i)
        cnt = jnp.sum(jnp.where(key_ref[...] >= cand, 1.0, 0.0), axis=1, keepdims=True)
        return jnp.where(cnt >= float(topk), cand, t)

    thr = lax.fori_loop(0, 32, thr_body, jnp.full((nq, 1), INT_MIN, jnp.int32))

    key = key_ref[...]
    above = key > thr
    tied = key == thr
    need = float(topk) - jnp.sum(jnp.where(above, 1.0, 0.0), axis=1, keepdims=True)
    n_tied = jnp.sum(jnp.where(tied, 1.0, 0.0), axis=1, keepdims=True)
    key_neg_inf = jnp.int32(np.array(-np.inf, np.float32).view(np.int32) ^ 0x7FFFFFFF)
    tie_rows = (n_tied > need) & (thr > key_neg_inf)
    any_tie = jnp.max(jnp.where(tie_rows, 1.0, 0.0)) > 0.0

    def tie_cut():
        def cut_body(i, j):
            cand = j + lax.shift_left(jnp.int32(1), idx_bits - 1 - i)
            cnt = jnp.sum(jnp.where((key_ref[...] == thr) & (kidx < cand), 1.0, 0.0), axis=1, keepdims=True)
            return jnp.where(cnt <= need, cand, j)
        return lax.fori_loop(0, idx_bits, cut_body, jnp.zeros((nq, 1), jnp.int32))

    cut = lax.cond(any_tie, tie_cut, lambda: jnp.full((nq, 1), 2 ** idx_bits, jnp.int32))
    sel = (above | (tied & (kidx < cut))) & adm & (score < jnp.float32(jnp.inf))
    bias = jnp.where(sel, 0.0, neg_inf)

    groups = DSA_H // DSA_KVH
    for j in range(DSA_KVH):
        kb = k_ref[:, j * DSA_HD:(j + 1) * DSA_HD].astype(BF16)
        vb = v_ref[:, j * DSA_HD:(j + 1) * DSA_HD].astype(BF16)
        for g in range(groups):
            h = j * groups + g
            qh = q_ref[:, h * DSA_HD:(h + 1) * DSA_HD].astype(BF16)
            s = _dot_nt(qh, kb) * (DSA_HD ** -0.5) + bias
            m = jnp.max(s, axis=1, keepdims=True)
            p = jnp.exp(s - m)
            l = jnp.sum(p, axis=1, keepdims=True)
            o = _dot(p.astype(BF16), vb) / l
            o_ref[:, h * DSA_HD:(h + 1) * DSA_HD] = o.astype(o_ref.dtype)


def dsa_attend(q_src, qi_src, misc_src, k_src, v_src, ki_src, *, nb, seq, row0, nq,
               n_keys, kv_from_stream, causal_chunk, n_valid, topk):
    nqb = seq // nq
    rb0 = row0 // nq
    qmap = lambda cb: (lambda b, t: (rb0 + b * nqb + t, cb))
    in_specs = [pl.BlockSpec((nq, DSA_H * DSA_HD), qmap(0)),
                pl.BlockSpec((nq, IDX_H * IDX_D), qmap(0)),
                pl.BlockSpec((nq, LANE), qmap(G_MISC // LANE))]
    if kv_from_stream:
        kb0 = row0 // n_keys
        kw = DSA_KVH * DSA_HD
        in_specs += [pl.BlockSpec((n_keys, kw), lambda b, t: (kb0 + b, DSA_H * DSA_HD // kw)),
                     pl.BlockSpec((n_keys, kw), lambda b, t: (kb0 + b, G_DV // kw)),
                     pl.BlockSpec((n_keys, LANE), lambda b, t: (kb0 + b, IDX_H * IDX_D // LANE))]
    else:
        in_specs += [pl.BlockSpec((None, n_keys, DSA_KVH * DSA_HD), lambda b, t: (b, 0, 0)),
                     pl.BlockSpec((None, n_keys, DSA_KVH * DSA_HD), lambda b, t: (b, 0, 0)),
                     pl.BlockSpec((None, n_keys, LANE), lambda b, t: (b, 0, 0))]
    idx_bits = int(np.ceil(np.log2(n_keys + 1)))
    return pl.pallas_call(
        functools.partial(_dsa_kernel, nq=nq, n_keys=n_keys, causal_chunk=causal_chunk,
                          n_valid=n_valid, topk=topk, idx_bits=idx_bits),
        grid=(nb, nqb),
        in_specs=in_specs,
        out_specs=pl.BlockSpec((nq, DSA_H * DSA_HD), lambda b, t: (b * nqb + t, 0)),
        out_shape=jax.ShapeDtypeStruct((nb * seq, DSA_H * DSA_HD), BF16),
        scratch_shapes=[pltpu.VMEM((nq, n_keys), jnp.int32)],
        compiler_params=_cparams(("parallel", "arbitrary")),
        name="dsa",
    )(q_src, qi_src, misc_src, k_src, v_src, ki_src)


def _xattn_kernel(q_ref, mk_ref, mv_ref, o_ref):
    for h in range(XA_H):
        cols = slice(h * XA_HD, (h + 1) * XA_HD)
        s = _dot_nt(q_ref[:, cols], mk_ref[:, cols].astype(BF16)) * (XA_HD ** -0.5)
        m = jnp.max(s, axis=1, keepdims=True)
        p = jnp.exp(s - m)
        l = jnp.sum(p, axis=1, keepdims=True)
        o = _dot(p.astype(BF16), mv_ref[:, cols].astype(BF16)) / l
        o_ref[:, cols] = o.astype(o_ref.dtype)


def cross_attend(q, mk, mv, *, nb, seq, row0, tq):
    nt = seq // tq
    rb0 = row0 // tq
    d = q.shape[1]
    return pl.pallas_call(
        _xattn_kernel,
        grid=(nb, nt),
        in_specs=[pl.BlockSpec((tq, d), lambda b, t: (rb0 + b * nt + t, 0)),
                  pl.BlockSpec((None, N_MEM, d), lambda b, t: (b, 0, 0)),
                  pl.BlockSpec((None, N_MEM, d), lambda b, t: (b, 0, 0))],
        out_specs=pl.BlockSpec((tq, d), lambda b, t: (b * nt + t, 0)),
        out_shape=jax.ShapeDtypeStruct((nb * seq, d), BF16),
        compiler_params=_cparams(("parallel", "parallel")),
        name="xattn",
    )(q, mk, mv)


def _top2_sum(vals):
    a, b, c, d = vals
    m1, n1 = jnp.maximum(a, b), jnp.minimum(a, b)
    m2, n2 = jnp.maximum(c, d), jnp.minimum(c, d)
    return jnp.maximum(m1, m2) + jnp.maximum(jnp.minimum(m1, m2), jnp.maximum(n1, n2))


def _router_kernel(x_ref, rwt_ref, rb_ref, comb_ref):
    logits = lax.dot_general(rwt_ref[...], x_ref[...], (((1,), (1,)), ((), ())),
                             preferred_element_type=F32, precision=lax.Precision.HIGHEST)
    aff = jax.nn.sigmoid(logits)
    biased = aff + rb_ref[...]
    a = [aff[e:e + 1, :] for e in range(N_EXPERTS)]
    b = [biased[e:e + 1, :] for e in range(N_EXPERTS)]
    gs = [_top2_sum(b[g * EXP_PER_GROUP:(g + 1) * EXP_PER_GROUP]) for g in range(N_GROUPS)]
    one = jnp.ones_like(a[0])
    zero = jnp.zeros_like(a[0])
    gated = []
    for g in range(N_GROUPS):
        win = one
        for g2 in range(N_GROUPS):
            if g2 < g:
                win = win * jnp.where(gs[g] > gs[g2], one, zero)
            elif g2 > g:
                win = win * jnp.where(gs[g] >= gs[g2], one, zero)
        for e in range(g * EXP_PER_GROUP, (g + 1) * EXP_PER_GROUP):
            beaten = zero
            for f in range(g * EXP_PER_GROUP, (g + 1) * EXP_PER_GROUP):
                if f < e:
                    beaten = beaten + jnp.where(b[f] >= b[e], one, zero)
                elif f > e:
                    beaten = beaten + jnp.where(b[f] > b[e], one, zero)
            gated.append(jnp.where(beaten < 2.0, win, zero) * a[e])
    denom = gated[0]
    for t in gated[1:]:
        denom = denom + t
    comb_ref[...] = jnp.concatenate(gated, axis=0) / denom


def moe_router(x, rwt, rb, *, tm):
    n, d = x.shape
    return pl.pallas_call(
        _router_kernel,
        grid=(n // tm,),
        in_specs=[pl.BlockSpec((tm, d), lambda i: (i, 0)),
                  pl.BlockSpec((N_EXPERTS, d), lambda i: (0, 0)),
                  pl.BlockSpec((N_EXPERTS, 1), lambda i: (0, 0))],
        out_specs=pl.BlockSpec((N_EXPERTS, tm), lambda i: (0, i)),
        out_shape=jax.ShapeDtypeStruct((N_EXPERTS, n), F32),
        compiler_params=_cparams(("parallel",)),
        name="router",
    )(x, rwt, rb)


def _moe_kernel(xb_ref, x_ref, comb_ref, wg_ref, wu_ref, wd_ref, g_ref, b_ref, o_ref, ob_ref, acc_ref):
    e = pl.program_id(1)

    @pl.when(e == 0)
    def _():
        acc_ref[...] = jnp.zeros_like(acc_ref)

    xb = xb_ref[...]
    hg = _dot(xb, wg_ref[...])
    hu = _dot(xb, wu_ref[...])
    comb = comb_ref[...]
    lane = lax.broadcasted_iota(jnp.int32, comb.shape, 1)
    c = jnp.sum(jnp.where(lane == e, comb, 0.0), axis=1, keepdims=True)
    hid = (hg * jax.nn.sigmoid(hg)) * hu * c
    acc_ref[...] += _dot(hid.astype(BF16), wd_ref[...])

    @pl.when(e == pl.num_programs(1) - 1)
    def _():
        y = _layer_norm_rows(DN_ALPHA * x_ref[...] + acc_ref[...], g_ref[...], b_ref[...])
        o_ref[...] = y
        ob_ref[...] = y.astype(BF16)


def moe_ffn_ln(xb, x, comb, wg, wu, wd, g, b, *, tm):
    n, d = x.shape
    return pl.pallas_call(
        _moe_kernel,
        grid=(n // tm, N_EXPERTS),
        in_specs=[pl.BlockSpec((tm, d), lambda i, e: (i, 0)),
                  pl.BlockSpec((tm, d), lambda i, e: (i, 0)),
                  pl.BlockSpec((tm, N_EXPERTS), lambda i, e: (i, 0)),
                  pl.BlockSpec((None, d, D_FF), lambda i, e: (e, 0, 0)),
                  pl.BlockSpec((None, d, D_FF), lambda i, e: (e, 0, 0)),
                  pl.BlockSpec((None, D_FF, d), lambda i, e: (e, 0, 0)),
                  pl.BlockSpec((1, d), lambda i, e: (0, 0)),
                  pl.BlockSpec((1, d), lambda i, e: (0, 0))],
        out_specs=[pl.BlockSpec((tm, d), lambda i, e: (i, 0)),
                   pl.BlockSpec((tm, d), lambda i, e: (i, 0))],
        out_shape=[jax.ShapeDtypeStruct((n, d), F32), jax.ShapeDtypeStruct((n, d), BF16)],
        scratch_shapes=[pltpu.VMEM((tm, d), F32)],
        compiler_params=_cparams(("parallel", "arbitrary")),
        name="moe",
    )(xb, x, comb, wg, wu, wd, g.reshape(1, d), b.reshape(1, d))


def _rope_tables(pos, head_dim):
    half = head_dim // 2
    inv_freq = ROPE_THETA ** (-jnp.arange(half, dtype=F32) / half)
    ang = pos.astype(F32)[:, None] * inv_freq[None, :]
    cos, sin = jnp.cos(ang), jnp.sin(ang)
    reps = LANE // head_dim
    return (jnp.tile(jnp.concatenate([cos, cos], -1), (1, reps)),
            jnp.tile(jnp.concatenate([-sin, sin], -1), (1, reps)))


def _pack_w_in(w):
    gq, gk, gv, glr, gog, dq, dk, dv, iq, ik, iw = jnp.split(w, SPLIT_POINTS, axis=-1)
    pad = jnp.zeros((w.shape[0], LANE - GATE_RANK - IDX_H), w.dtype)
    wg = jnp.concatenate([gq, gk, gv, gog, dv, glr, iw, pad], -1).astype(BF16)
    wr = jnp.concatenate([dq, dk], -1).astype(BF16)
    wi = jnp.concatenate([iq, ik, ik], -1).astype(BF16)
    return wg, wr, wi


def _pack_wa2(wa2):
    w = wa2.reshape(GATE_RANK, GLA_H, GLA_DK).transpose(1, 0, 2)
    return jnp.pad(w, ((0, 0), (MISC_LR, LANE - GATE_RANK - MISC_LR), (0, 0)))


def kernel(x_prompt, x_sample, cache_dsa_k, cache_dsa_v, cache_idx_k, state_gla, cache_mem_k, cache_mem_v,
           mem_prompt, ln_in_g, ln_in_b, w_in, gla_wa2, gla_ba, gla_norm_g, w_out, ln1_g, ln1_b,
           xa_wq, xa_wk, xa_wv, xa_wo, ln2_g, ln2_b, router_w, router_b,
           moe_w_gate, moe_w_up, moe_w_down, ln3_g, ln3_b):
    tm = 768
    x_all = jnp.concatenate([x_prompt.reshape(N_PROMPT, D_MODEL), x_sample.reshape(N_SAMPLE, D_MODEL)], 0)
    x, xb = layer_norm_rows(x_all, ln_in_g, ln_in_b, tm)

    pos = jnp.concatenate([jnp.tile(jnp.arange(SEQ), BATCH),
                           jnp.tile(PAST_LEN + jnp.arange(DEC_SEQ), DEC_BATCH)])
    rope128 = _rope_tables(pos, DSA_HD) + (DSA_HD // 2,)
    rope64 = _rope_tables(pos, IDX_D) + (IDX_D // 2,)
    memb = mem_prompt.reshape(BATCH * N_MEM, D_MODEL).astype(BF16)
    rwt = router_w.T
    rb = router_b.reshape(N_EXPERTS, 1)
    n_keys_s = PAST_LEN + DEC_SEQ
    n_keys_pad = -(-n_keys_s // LANE) * LANE

    p_states, s_states = [], []
    for l in range(DEPTH):
        wg, wr, wi = _pack_w_in(w_in[l])
        pg = matmul(xb, wg, tm=tm, tn=1152, name="in_proj_g")
        pr = matmul(xb, wr, tm=tm, tn=WR_COLS, rope=rope128, name="in_proj_r")
        pi = matmul(xb, wi, tm=tm, tn=WI_COLS, rope=rope64, name="in_proj_i")

        wa2p = _pack_wa2(gla_wa2[l])
        ba = gla_ba[l].reshape(GLA_H, 1, GLA_DK)
        gn = gla_norm_g[l].reshape(1, GLA_DV)
        mixg_p, st_p = gla_mix(pg, wa2p, ba, gn, None, nb=BATCH, seq=SEQ, row0=0,
                               tb=256, chunk=CHUNK, sub=GLA_SUB)
        mixg_s, st_s = gla_mix(pg, wa2p, ba, gn, state_gla[l], nb=DEC_BATCH, seq=DEC_SEQ, row0=N_PROMPT,
                               tb=DEC_SEQ, chunk=DEC_SEQ, sub=DEC_SEQ)

        od_p = dsa_attend(pr, pi, pg, pr, pg, pi, nb=BATCH, seq=SEQ, row0=0, nq=128, n_keys=SEQ,
                          kv_from_stream=True, causal_chunk=CHUNK, n_valid=SEQ, topk=min(TOPK_MAX, SEQ // 4))
        k_new = pr[N_PROMPT:, DSA_H * DSA_HD:].reshape(DEC_BATCH, DEC_SEQ, DSA_KVH * DSA_HD)
        v_new = pg[N_PROMPT:, G_DV:G_DV + DSA_KVH * DSA_HD].reshape(DEC_BATCH, DEC_SEQ, DSA_KVH * DSA_HD)
        ki_new = pi[N_PROMPT:, IDX_H * IDX_D:].reshape(DEC_BATCH, DEC_SEQ, LANE)
        kpad = ((0, 0), (0, n_keys_pad - n_keys_s), (0, 0))
        ck = cache_dsa_k[l].reshape(DEC_BATCH, PAST_LEN, DSA_KVH * DSA_HD)
        cv = cache_dsa_v[l].reshape(DEC_BATCH, PAST_LEN, DSA_KVH * DSA_HD)
        cik = jnp.concatenate([cache_idx_k[l], cache_idx_k[l]], -1)
        k_all = jnp.pad(jnp.concatenate([ck, k_new], 1), kpad)
        v_all = jnp.pad(jnp.concatenate([cv, v_new], 1), kpad)
        ki_all = jnp.pad(jnp.concatenate([cik, ki_new], 1), kpad)
        od_s = dsa_attend(pr, pi, pg, k_all, v_all, ki_all, nb=DEC_BATCH, seq=DEC_SEQ, row0=N_PROMPT,
                          nq=DEC_SEQ, n_keys=n_keys_pad, kv_from_stream=False, causal_chunk=None,
                          n_valid=n_keys_s, topk=min(TOPK_MAX, n_keys_s // 4))

        mixg = jnp.concatenate([mixg_p, mixg_s], 0)
        od = jnp.concatenate([od_p, od_s], 0)
        wo = w_out[l].astype(BF16)
        x1, x1b = matmul_ln([mixg, od], [wo[:GLA_H * GLA_DV], wo[GLA_H * GLA_DV:]], x, ln1_g[l], ln1_b[l],
                            tm=384, name="out_proj_ln1")

        q_xa = matmul(x1b, xa_wq[l].astype(BF16), tm=tm, tn=1024, out_dtype=BF16, name="xa_q")
        mk_p = matmul(memb, xa_wk[l].astype(BF16), tm=512, tn=1024, name="mem_k")
        mv_p = matmul(memb, xa_wv[l].astype(BF16), tm=512, tn=1024, name="mem_v")
        xa_p = cross_attend(q_xa, mk_p.reshape(BATCH, N_MEM, D_MODEL), mv_p.reshape(BATCH, N_MEM, D_MODEL),
                            nb=BATCH, seq=SEQ, row0=0, tq=512)
        xa_s = cross_attend(q_xa, cache_mem_k[l].reshape(DEC_BATCH, N_MEM, D_MODEL),
                            cache_mem_v[l].reshape(DEC_BATCH, N_MEM, D_MODEL),
                            nb=DEC_BATCH, seq=DEC_SEQ, row0=N_PROMPT, tq=DEC_SEQ)
        xa = jnp.concatenate([xa_p, xa_s], 0)
        x2, x2b = matmul_ln([xa], [xa_wo[l].astype(BF16)], x1, ln2_g[l], ln2_b[l], tm=384, name="xa_o_ln2")

        comb = moe_router(x2, rwt, rb, tm=tm).T
        x, xb = moe_ffn_ln(x2b, x2, comb, moe_w_gate[l].astype(BF16), moe_w_up[l].astype(BF16),
                           moe_w_down[l].astype(BF16), ln3_g[l], ln3_b[l], tm=384)

        p_states.append((pr[:N_PROMPT, DSA_H * DSA_HD:].reshape(BATCH, SEQ, DSA_KVH, DSA_HD),
                         pg[:N_PROMPT, G_DV:G_DV + DSA_KVH * DSA_HD].reshape(BATCH, SEQ, DSA_KVH, DSA_HD),
                         pi[:N_PROMPT, IDX_H * IDX_D:IDX_H * IDX_D + IDX_D].reshape(BATCH, SEQ, IDX_D),
                         st_p,
                         mk_p.reshape(BATCH, N_MEM, XA_H, XA_HD),
                         mv_p.reshape(BATCH, N_MEM, XA_H, XA_HD)))
        s_states.append((k_new.reshape(DEC_BATCH, DEC_SEQ, DSA_KVH, DSA_HD),
                         v_new.reshape(DEC_BATCH, DEC_SEQ, DSA_KVH, DSA_HD),
                         ki_new[:, :, :IDX_D],
                         st_s))

    def stack(states, i):
        return jnp.stack([st[i] for st in states])

    return (x[:N_PROMPT].reshape(BATCH, SEQ, D_MODEL), x[N_PROMPT:].reshape(DEC_BATCH, DEC_SEQ, D_MODEL),
            stack(p_states, 0), stack(p_states, 1), stack(p_states, 2), stack(p_states, 3),
            stack(p_states, 4), stack(p_states, 5),
            stack(s_states, 0), stack(s_states, 1), stack(s_states, 2), stack(s_states, 3))
```

```python
import functools

import jax
import jax.numpy as jnp
import numpy as np
from jax import lax
from jax.experimental import pallas as pl
from jax.experimental.pallas import tpu as pltpu

F32 = jnp.float32
BF16 = jnp.bfloat16

D_MODEL = 2048
BATCH = 4
SEQ = 2048
DEPTH = 2
DEC_BATCH = 16
DEC_SEQ = 16
PAST_LEN = 4096
CHUNK = 64
N_MEM = 256
GLA_H = 4
GLA_DK = 128
GLA_DV = 256
GATE_RANK = 16
GATE_TAU = 16.0
DSA_H = 8
DSA_KVH = 2
DSA_HD = 128
IDX_H = 16
IDX_D = 64
TOPK_MAX = 256
XA_H = 4
XA_HD = D_MODEL // XA_H
N_EXPERTS = 16
N_GROUPS = 4
EXP_PER_GROUP = N_EXPERTS // N_GROUPS
D_FF = 512
ROPE_THETA = 10000.0
LN_EPS = 1e-5
DN_ALPHA = (2 * DEPTH) ** 0.25

IN_SPLITS = (GLA_H * GLA_DK, GLA_H * GLA_DK, GLA_H * GLA_DV, GATE_RANK, GLA_H * GLA_DV,
             DSA_H * DSA_HD, DSA_KVH * DSA_HD, DSA_KVH * DSA_HD, IDX_H * IDX_D, IDX_D, IDX_H)
SPLIT_POINTS = tuple(int(c) for c in np.cumsum(IN_SPLITS)[:-1])

N_PROMPT = BATCH * SEQ
N_SAMPLE = DEC_BATCH * DEC_SEQ
N_TOK = N_PROMPT + N_SAMPLE

LANE = 128
GLA_SUB = 16
VMEM_LIMIT = 48 * 1024 * 1024

G_Q, G_K, G_V, G_OG, G_DV, G_MISC = 0, 512, 1024, 2048, 3072, 3328
WG_COLS = 3456
WR_COLS = 1280
WI_COLS = 1152
MISC_LR = 0
MISC_IW = GATE_RANK


def _cparams(sem):
    return pltpu.CompilerParams(dimension_semantics=sem, vmem_limit_bytes=VMEM_LIMIT)


def _dot(a, b):
    return jnp.dot(a, b, preferred_element_type=F32)


def _dot_nt(a, b):
    return lax.dot_general(a, b, (((1,), (1,)), ((), ())), preferred_element_type=F32)


def _dot_tn(a, b):
    return lax.dot_general(a, b, (((0,), (0,)), ((), ())), preferred_element_type=F32)


def _layer_norm_rows(y, g, b):
    mu = jnp.mean(y, axis=-1, keepdims=True)
    d = y - mu
    var = jnp.mean(d * d, axis=-1, keepdims=True)
    return d * lax.rsqrt(var + LN_EPS) * g + b


def _seg_specs(segs, tm):
    specs, starts, start = [], [], 0
    for a in segs:
        nt = a.shape[0] // tm
        specs.append(pl.BlockSpec((tm, a.shape[1]),
                                  functools.partial(lambda i, s0, n: (jnp.clip(i - s0, 0, n - 1), 0), s0=start, n=nt)))
        starts.append(start)
        start += nt
    return specs, tuple(starts), start


def _seg_load(refs, starts):
    i = pl.program_id(0)
    v = refs[0][...]
    for r, s0 in zip(refs[1:], starts[1:]):
        v = jnp.where(i >= s0, r[...], v)
    return v


def _ln_kernel(*refs, starts):
    n = len(starts)
    g_ref, b_ref, o_ref, ob_ref = refs[n:]
    y = _layer_norm_rows(_seg_load(refs[:n], starts), g_ref[...], b_ref[...])
    o_ref[...] = y
    ob_ref[...] = y.astype(BF16)


def layer_norm_rows(segs, g, b, tm):
    d = segs[0].shape[1]
    specs, starts, nt = _seg_specs(segs, tm)
    return pl.pallas_call(
        functools.partial(_ln_kernel, starts=starts),
        grid=(nt,),
        in_specs=specs + [pl.BlockSpec((1, d), lambda i: (0, 0)),
                          pl.BlockSpec((1, d), lambda i: (0, 0))],
        out_specs=[pl.BlockSpec((tm, d), lambda i: (i, 0)),
                   pl.BlockSpec((tm, d), lambda i: (i, 0))],
        out_shape=[jax.ShapeDtypeStruct((nt * tm, d), F32), jax.ShapeDtypeStruct((nt * tm, d), BF16)],
        compiler_params=_cparams(("parallel",)),
        name="ln_in",
    )(*segs, g.reshape(1, d), b.reshape(1, d))


def _swap_halves(x, half):
    if half == LANE // 2:
        return pltpu.roll(x, half, 1)
    lane = lax.broadcasted_iota(jnp.int32, x.shape, 1)
    lower = (lane % (2 * half)) < half
    return jnp.where(lower, pltpu.roll(x, LANE - half, 1), pltpu.roll(x, half, 1))


def _mm_kernel(a_ref, w_ref, *rest, rope_half, tn):
    acc = _dot(a_ref[...], w_ref[...])
    if rope_half is None:
        (o_ref,) = rest
        o_ref[...] = acc.astype(o_ref.dtype)
        return
    cos_ref, sin_ref, o_ref = rest
    cos = cos_ref[...]
    sin = sin_ref[...]
    for c in range(tn // LANE):
        x = acc[:, c * LANE:(c + 1) * LANE]
        o_ref[:, c * LANE:(c + 1) * LANE] = (x * cos + _swap_halves(x, rope_half) * sin).astype(o_ref.dtype)


def matmul(a, w, *, tm, tn, out_dtype=F32, rope=None, name="mm"):
    m, k = a.shape
    n = w.shape[1]
    in_specs = [pl.BlockSpec((tm, k), lambda j, i: (i, 0)),
                pl.BlockSpec((k, tn), lambda j, i: (0, j))]
    args = [a, w]
    rope_half = None
    if rope is not None:
        cos, sin, rope_half = rope
        in_specs += [pl.BlockSpec((tm, LANE), lambda j, i: (i, 0)),
                     pl.BlockSpec((tm, LANE), lambda j, i: (i, 0))]
        args += [cos, sin]
    return pl.pallas_call(
        functools.partial(_mm_kernel, rope_half=rope_half, tn=tn),
        grid=(n // tn, m // tm),
        in_specs=in_specs,
        out_specs=pl.BlockSpec((tm, tn), lambda j, i: (i, j)),
        out_shape=jax.ShapeDtypeStruct((m, n), out_dtype),
        compiler_params=_cparams(("parallel", "parallel")),
        name=name,
    )(*args)


def _mm_ln_kernel(*refs, seg_starts):
    pos = 0
    a_vals = []
    for starts in seg_starts:
        a_vals.append(_seg_load(refs[pos:pos + len(starts)], starts))
        pos += len(starts)
    n_pairs = len(seg_starts)
    w_refs = refs[pos:pos + n_pairs]
    resid_ref, g_ref, b_ref, o_ref, ob_ref = refs[pos + n_pairs:]
    acc = _dot(a_vals[0], w_refs[0][...])
    for a, w_ref in zip(a_vals[1:], w_refs[1:]):
        acc = acc + _dot(a, w_ref[...])
    y = _layer_norm_rows(DN_ALPHA * resid_ref[...] + acc, g_ref[...], b_ref[...])
    o_ref[...] = y
    ob_ref[...] = y.astype(BF16)


def matmul_ln(a_list, w_list, resid, g, b, *, tm, name):
    m, d = resid.shape
    a_specs, seg_starts, a_args = [], [], []
    for segs in a_list:
        specs, starts, nt = _seg_specs(segs, tm)
        assert nt * tm == m
        a_specs += specs
        seg_starts.append(starts)
        a_args += list(segs)
    in_specs = (a_specs
                + [pl.BlockSpec(w.shape, lambda i: (0, 0)) for w in w_list]
                + [pl.BlockSpec((tm, d), lambda i: (i, 0)),
                   pl.BlockSpec((1, d), lambda i: (0, 0)),
                   pl.BlockSpec((1, d), lambda i: (0, 0))])
    return pl.pallas_call(
        functools.partial(_mm_ln_kernel, seg_starts=tuple(seg_starts)),
        grid=(m // tm,),
        in_specs=in_specs,
        out_specs=[pl.BlockSpec((tm, d), lambda i: (i, 0)),
                   pl.BlockSpec((tm, d), lambda i: (i, 0))],
        out_shape=[jax.ShapeDtypeStruct((m, d), F32), jax.ShapeDtypeStruct((m, d), BF16)],
        compiler_params=_cparams(("parallel",)),
        name=name,
    )(*a_args, *w_list, resid, g.reshape(1, d), b.reshape(1, d))


def _log_sigmoid(z):
    return jnp.minimum(z, 0.0) - jnp.log1p(jnp.exp(-jnp.abs(z)))


def _gla_kernel(*refs, tb, chunk, sub, has_state):
    if has_state:
        q_ref, k_ref, v_ref, og_ref, misc_ref, wa2_ref, ba_ref, gn_ref, s0_ref, o_ref, sout_ref, st_ref = refs
    else:
        q_ref, k_ref, v_ref, og_ref, misc_ref, wa2_ref, ba_ref, gn_ref, o_ref, sout_ref, st_ref = refs
    t = pl.program_id(2)

    @pl.when(t == 0)
    def _():
        if has_state:
            st_ref[...] = s0_ref[...].T
        else:
            st_ref[...] = jnp.zeros_like(st_ref)

    z = jnp.dot(misc_ref[...], wa2_ref[...], preferred_element_type=F32,
                precision=lax.Precision.HIGHEST) + ba_ref[...]
    g_all = _log_sigmoid(z) * (1.0 / GATE_TAU)

    row = lax.broadcasted_iota(jnp.int32, (chunk, chunk), 0)
    col = lax.broadcasted_iota(jnp.int32, (chunk, chunk), 1)
    tril = (row >= col).astype(F32)
    gn = gn_ref[...]

    for c in range(tb // chunk):
        rows = slice(c * chunk, (c + 1) * chunk)
        G = jnp.dot(tril, g_all[rows], preferred_element_type=F32,
                    precision=lax.Precision.HIGHEST)
        qc = q_ref[rows, :] * (GLA_DK ** -0.5)
        kc = k_ref[rows, :]
        vb = v_ref[rows, :].astype(BF16)
        st = st_ref[...]
        o = _dot_nt((qc * jnp.exp(G)).astype(BF16), st.astype(BF16))
        o_parts = []
        for i in range(chunk // sub):
            r0 = i * sub
            nk = r0 + sub
            b_i = G[r0:r0 + 1, :]
            qt = qc[r0:nk] * jnp.exp(G[r0:nk] - b_i)
            kt = kc[:nk] * jnp.exp(b_i - G[:nk])
            a = _dot_nt(qt.astype(BF16), kt.astype(BF16))
            t_idx = r0 + lax.broadcasted_iota(jnp.int32, (sub, nk), 0)
            s_idx = lax.broadcasted_iota(jnp.int32, (sub, nk), 1)
            a = jnp.where(t_idx >= s_idx, a, 0.0)
            o_parts.append(_dot(a.astype(BF16), vb[:nk]))
        o = o + jnp.concatenate(o_parts, axis=0)
        g_last = G[chunk - 1:chunk, :]
        k_dec = kc * jnp.exp(g_last - G)
        st_ref[...] = st * jnp.exp(g_last) + _dot_tn(vb, k_dec.astype(BF16))
        n = o * lax.rsqrt(jnp.mean(o * o, axis=-1, keepdims=True) + LN_EPS) * gn
        og = og_ref[rows, :]
        o_ref[rows, :] = (n * (og * jax.nn.sigmoid(og))).astype(o_ref.dtype)

    @pl.when(t == pl.num_programs(2) - 1)
    def _():
        sout_ref[...] = st_ref[...].T


def gla_mix(pg, wa2p, ba, gn, s0, *, nb, seq, row0, tb, chunk, sub):
    nt = seq // tb
    rb0 = row0 // tb

    def tok(colblock):
        return lambda b, h, t: (rb0 + b * nt + t, colblock(h))

    in_specs = [
        pl.BlockSpec((tb, GLA_DK), tok(lambda h: G_Q // GLA_DK + h)),
        pl.BlockSpec((tb, GLA_DK), tok(lambda h: G_K // GLA_DK + h)),
        pl.BlockSpec((tb, GLA_DV), tok(lambda h: G_V // GLA_DV + h)),
        pl.BlockSpec((tb, GLA_DV), tok(lambda h: G_OG // GLA_DV + h)),
        pl.BlockSpec((tb, LANE), tok(lambda h: G_MISC // LANE)),
        pl.BlockSpec((None, LANE, GLA_DK), lambda b, h, t: (h, 0, 0)),
        pl.BlockSpec((None, 1, GLA_DK), lambda b, h, t: (h, 0, 0)),
        pl.BlockSpec((1, GLA_DV), lambda b, h, t: (0, 0)),
    ]
    args = [pg, pg, pg, pg, pg, wa2p, ba, gn]
    if s0 is not None:
        in_specs.append(pl.BlockSpec((None, None, GLA_DK, GLA_DV), lambda b, h, t: (b, h, 0, 0)))
        args.append(s0)
    return pl.pallas_call(
        functools.partial(_gla_kernel, tb=tb, chunk=chunk, sub=sub, has_state=s0 is not None),
        grid=(nb, GLA_H, nt),
        in_specs=in_specs,
        out_specs=[pl.BlockSpec((tb, GLA_DV), lambda b, h, t: (b * nt + t, h)),
                   pl.BlockSpec((None, None, GLA_DK, GLA_DV), lambda b, h, t: (b, h, 0, 0))],
        out_shape=[jax.ShapeDtypeStruct((nb * seq, GLA_H * GLA_DV), BF16),
                   jax.ShapeDtypeStruct((nb, GLA_H, GLA_DK, GLA_DV), F32)],
        scratch_shapes=[pltpu.VMEM((GLA_DV, GLA_DK), F32)],
        compiler_params=_cparams(("parallel", "parallel", "arbitrary")),
        name="gla",
    )(*args)


INT_MIN = -2 ** 31
NEG_INF = float("-inf")


def _order_key(score):
    bits = pltpu.bitcast(score, jnp.int32)
    return jnp.where(bits < 0, bits ^ jnp.int32(0x7FFFFFFF), bits)


def _count(mask):
    return jnp.sum(jnp.where(mask, 1.0, 0.0), axis=1, keepdims=True)


def _topk_bias(score, adm, kidx, key_ref, topk):
    nq, n = score.shape
    idx_bits = int(np.ceil(np.log2(n + 1)))
    score = jnp.where(adm, score, NEG_INF)
    key_ref[...] = _order_key(score)

    def thr_body(i, t):
        cand = t + lax.shift_left(jnp.int32(1), 31 - i)
        return jnp.where(_count(key_ref[...] >= cand) >= float(topk), cand, t)

    thr = lax.fori_loop(0, 32, thr_body, jnp.full((nq, 1), INT_MIN, jnp.int32))

    key = key_ref[...]
    above = key > thr
    tied = key == thr
    need = float(topk) - _count(above)
    key_neg_inf = jnp.int32(np.array(-np.inf, np.float32).view(np.int32) ^ 0x7FFFFFFF)
    tie_rows = (_count(tied) > need) & (thr > key_neg_inf)
    any_tie = jnp.max(jnp.where(tie_rows, 1.0, 0.0)) > 0.0

    def tie_cut():
        def cut_body(i, j):
            cand = j + lax.shift_left(jnp.int32(1), idx_bits - 1 - i)
            return jnp.where(_count((key_ref[...] == thr) & (kidx < cand)) <= need, cand, j)
        return lax.fori_loop(0, idx_bits, cut_body, jnp.zeros((nq, 1), jnp.int32))

    cut = lax.cond(any_tie, tie_cut, lambda: jnp.full((nq, 1), 2 ** idx_bits, jnp.int32))
    sel = (above | (tied & (kidx < cut))) & adm & (score < float("inf"))
    return jnp.where(sel, 0.0, NEG_INF)


def _idx_weight(misc, h):
    return misc[:, MISC_IW + h:MISC_IW + h + 1] * (IDX_H ** -0.5 * IDX_D ** -0.5)


def _softmax_pv(s, vb):
    m = jnp.max(s, axis=1, keepdims=True)
    p = jnp.exp(s - m)
    l = jnp.sum(p, axis=1, keepdims=True)
    return _dot(p.astype(BF16), vb) / l


def _dsa_prompt_kernel(q_ref, qi_ref, misc_ref, k_ref, v_ref, ki_ref, o_ref, key_ref,
                       *, nq, seq, chunk, topk, step):
    qb = pl.program_id(1)
    lane = lax.broadcasted_iota(jnp.int32, (nq, LANE), 1)
    upper = lane >= IDX_D
    misc = misc_ref[...]
    groups = DSA_H // DSA_KVH

    def body(n):
        kib = ki_ref[pl.ds(0, n), :].astype(BF16)
        score = jnp.zeros((nq, n), F32)
        for h in range(IDX_H):
            pair = qi_ref[:, (h // 2) * LANE:(h // 2 + 1) * LANE]
            qm = jnp.where(upper == (h % 2 == 1), pair, 0.0).astype(BF16)
            score = score + jnp.maximum(_dot_nt(qm, kib), 0.0) * _idx_weight(misc, h)
        kidx = lax.broadcasted_iota(jnp.int32, (nq, n), 1)
        qpos = qb * nq + lax.broadcasted_iota(jnp.int32, (nq, 1), 0)
        limit = (lax.shift_right_logical(qpos, int(np.log2(chunk))) + 1) * chunk
        bias = _topk_bias(score, kidx < limit, kidx, key_ref.at[:, pl.ds(0, n)], topk)
        for j in range(DSA_KVH):
            kb = k_ref[pl.ds(0, n), j * DSA_HD:(j + 1) * DSA_HD].astype(BF16)
            vb = v_ref[pl.ds(0, n), j * DSA_HD:(j + 1) * DSA_HD].astype(BF16)
            for g in range(groups):
                h = j * groups + g
                qh = q_ref[:, h * DSA_HD:(h + 1) * DSA_HD].astype(BF16)
                s = _dot_nt(qh, kb) * (DSA_HD ** -0.5) + bias
                o_ref[:, h * DSA_HD:(h + 1) * DSA_HD] = _softmax_pv(s, vb).astype(o_ref.dtype)

    n_need = (qb + 1) * nq
    cls = (n_need + step - 1) // step - 1
    for c in range(seq // step):
        pl.when(cls == c)(functools.partial(body, (c + 1) * step))


def dsa_prompt(pr, pi, pg, *, nb, seq, nq, chunk, topk, step):
    nqb = seq // nq
    kw = DSA_KVH * DSA_HD
    qmap = lambda cb: (lambda b, t: (b * nqb + t, cb))
    return pl.pallas_call(
        functools.partial(_dsa_prompt_kernel, nq=nq, seq=seq, chunk=chunk, topk=topk, step=step),
        grid=(nb, nqb),
        in_specs=[pl.BlockSpec((nq, DSA_H * DSA_HD), qmap(0)),
                  pl.BlockSpec((nq, IDX_H * IDX_D), qmap(0)),
                  pl.BlockSpec((nq, LANE), qmap(G_MISC // LANE)),
                  pl.BlockSpec((seq, kw), lambda b, t: (b, DSA_H * DSA_HD // kw)),
                  pl.BlockSpec((seq, kw), lambda b, t: (b, G_DV // kw)),
                  pl.BlockSpec((seq, LANE), lambda b, t: (b, IDX_H * IDX_D // LANE))],
        out_specs=pl.BlockSpec((nq, DSA_H * DSA_HD), lambda b, t: (b * nqb + t, 0)),
        out_shape=jax.ShapeDtypeStruct((nb * seq, DSA_H * DSA_HD), BF16),
        scratch_shapes=[pltpu.VMEM((nq, seq), jnp.int32)],
        compiler_params=_cparams(("parallel", "arbitrary")),
        name="dsa_prompt",
    )(pr, pi, pg, pr, pg, pi)


def _dsa_sample_kernel(q_ref, qi_ref, misc_ref, kn_ref, vn_ref, kin_ref, ck_ref, cv_ref, cik_ref, o_ref, key_ref,
                       *, nq, past, topk):
    misc = misc_ref[...]
    groups = DSA_H // DSA_KVH
    zrows = LANE - nq
    cikb = cik_ref[...].astype(BF16)
    kinb = jnp.concatenate([kin_ref[:, :IDX_D], jnp.zeros((zrows, IDX_D), F32)], 0).astype(BF16)
    qs = jnp.concatenate([qi_ref[:, h * IDX_D:(h + 1) * IDX_D] for h in range(IDX_H)], 0).astype(BF16)
    lg_c = _dot_nt(qs, cikb)
    lg_n = _dot_nt(qs, kinb)
    sc_c = jnp.zeros((nq, past), F32)
    sc_n = jnp.zeros((nq, LANE), F32)
    for h in range(IDX_H):
        w = _idx_weight(misc, h)
        sc_c = sc_c + jnp.maximum(lg_c[h * nq:(h + 1) * nq], 0.0) * w
        sc_n = sc_n + jnp.maximum(lg_n[h * nq:(h + 1) * nq], 0.0) * w
    n = past + LANE
    score = jnp.concatenate([sc_c, sc_n], 1)
    kidx = lax.broadcasted_iota(jnp.int32, (nq, n), 1)
    bias = _topk_bias(score, kidx < past + nq, kidx, key_ref, topk)
    bias_g = jnp.concatenate([bias] * groups, 0)
    for j in range(DSA_KVH):
        cols = slice(j * DSA_HD, (j + 1) * DSA_HD)
        kb_c = ck_ref[:, j, :].astype(BF16)
        vb_c = cv_ref[:, j, :].astype(BF16)
        kb_n = jnp.concatenate([kn_ref[:, cols], jnp.zeros((zrows, DSA_HD), F32)], 0).astype(BF16)
        vb_n = jnp.concatenate([vn_ref[:, cols], jnp.zeros((zrows, DSA_HD), F32)], 0).astype(BF16)
        qg = jnp.concatenate([q_ref[:, (j * groups + g) * DSA_HD:(j * groups + g + 1) * DSA_HD]
                              for g in range(groups)], 0).astype(BF16)
        s = jnp.concatenate([_dot_nt(qg, kb_c), _dot_nt(qg, kb_n)], 1) * (DSA_HD ** -0.5) + bias_g
        m = jnp.max(s, axis=1, keepdims=True)
        p = jnp.exp(s - m)
        l = jnp.sum(p, axis=1, keepdims=True)
        pb = p.astype(BF16)
        o = (_dot(pb[:, :past], vb_c) + _dot(pb[:, past:], vb_n)) / l
        for g in range(groups):
            h = j * groups + g
            o_ref[:, h * DSA_HD:(h + 1) * DSA_HD] = o[g * nq:(g + 1) * nq].astype(o_ref.dtype)


def dsa_sample(pr, pi, pg, cache_k, cache_v, cache_ik, layer, *, nb, nq, row0, past, topk):
    rb0 = row0 // nq
    kw = DSA_KVH * DSA_HD
    smap = lambda cb: (lambda b: (rb0 + b, cb))
    return pl.pallas_call(
        functools.partial(_dsa_sample_kernel, nq=nq, past=past, topk=topk),
        grid=(nb,),
        in_specs=[pl.BlockSpec((nq, DSA_H * DSA_HD), smap(0)),
                  pl.BlockSpec((nq, IDX_H * IDX_D), smap(0)),
                  pl.BlockSpec((nq, LANE), smap(G_MISC // LANE)),
                  pl.BlockSpec((nq, kw), smap(DSA_H * DSA_HD // kw)),
                  pl.BlockSpec((nq, kw), smap(G_DV // kw)),
                  pl.BlockSpec((nq, LANE), smap(IDX_H * IDX_D // LANE)),
                  pl.BlockSpec((None, None, past, DSA_KVH, DSA_HD), lambda b: (layer, b, 0, 0, 0)),
                  pl.BlockSpec((None, None, past, DSA_KVH, DSA_HD), lambda b: (layer, b, 0, 0, 0)),
                  pl.BlockSpec((None, None, past, IDX_D), lambda b: (layer, b, 0, 0))],
        out_specs=pl.BlockSpec((nq, DSA_H * DSA_HD), lambda b: (b, 0)),
        out_shape=jax.ShapeDtypeStruct((nb * nq, DSA_H * DSA_HD), BF16),
        scratch_shapes=[pltpu.VMEM((nq, past + LANE), jnp.int32)],
        compiler_params=_cparams(("parallel",)),
        name="dsa_sample",
    )(pr, pi, pg, pr, pg, pi, cache_k, cache_v, cache_ik)


def _xattn_kernel(q_ref, mk_ref, mv_ref, o_ref, *, heads_split):
    for h in range(XA_H):
        cols = slice(h * XA_HD, (h + 1) * XA_HD)
        mk = mk_ref[:, h, :] if heads_split else mk_ref[:, cols]
        mv = mv_ref[:, h, :] if heads_split else mv_ref[:, cols]
        s = _dot_nt(q_ref[:, cols], mk.astype(BF16)) * (XA_HD ** -0.5)
        o_ref[:, cols] = _softmax_pv(s, mv.astype(BF16)).astype(o_ref.dtype)


def cross_attend(q, mk, mv, mem_index, mem_block, *, nb, seq, row0, tq):
    nt = seq // tq
    rb0 = row0 // tq
    d = q.shape[1]
    mem_spec = pl.BlockSpec(mem_block, lambda b, t: mem_index(b))
    return pl.pallas_call(
        functools.partial(_xattn_kernel, heads_split=len([s for s in mem_block if s is not None]) == 3),
        grid=(nb, nt),
        in_specs=[pl.BlockSpec((tq, d), lambda b, t: (rb0 + b * nt + t, 0)), mem_spec, mem_spec],
        out_specs=pl.BlockSpec((tq, d), lambda b, t: (b * nt + t, 0)),
        out_shape=jax.ShapeDtypeStruct((nb * seq, d), BF16),
        compiler_params=_cparams(("parallel", "parallel")),
        name="xattn",
    )(q, mk, mv)


def _top2_sum(vals):
    a, b, c, d = vals
    m1, n1 = jnp.maximum(a, b), jnp.minimum(a, b)
    m2, n2 = jnp.maximum(c, d), jnp.minimum(c, d)
    return jnp.maximum(m1, m2) + jnp.maximum(jnp.minimum(m1, m2), jnp.maximum(n1, n2))


def _router_kernel(x_ref, rwt_ref, rb_ref, comb_ref):
    logits = lax.dot_general(rwt_ref[...], x_ref[...], (((1,), (1,)), ((), ())),
                             preferred_element_type=F32, precision=lax.Precision.HIGHEST)
    aff = jax.nn.sigmoid(logits)
    biased = aff + rb_ref[...]
    a = [aff[e:e + 1, :] for e in range(N_EXPERTS)]
    b = [biased[e:e + 1, :] for e in range(N_EXPERTS)]
    gs = [_top2_sum(b[g * EXP_PER_GROUP:(g + 1) * EXP_PER_GROUP]) for g in range(N_GROUPS)]
    one = jnp.ones_like(a[0])
    zero = jnp.zeros_like(a[0])
    gated = []
    for g in range(N_GROUPS):
        win = one
        for g2 in range(N_GROUPS):
            if g2 < g:
                win = win * jnp.where(gs[g] > gs[g2], one, zero)
            elif g2 > g:
                win = win * jnp.where(gs[g] >= gs[g2], one, zero)
        for e in range(g * EXP_PER_GROUP, (g + 1) * EXP_PER_GROUP):
            beaten = zero
            for f in range(g * EXP_PER_GROUP, (g + 1) * EXP_PER_GROUP):
                if f < e:
                    beaten = beaten + jnp.where(b[f] >= b[e], one, zero)
                elif f > e:
                    beaten = beaten + jnp.where(b[f] > b[e], one, zero)
            gated.append(jnp.where(beaten < 2.0, win, zero) * a[e])
    denom = gated[0]
    for t in gated[1:]:
        denom = denom + t
    comb_ref[...] = jnp.concatenate(gated, axis=0) / denom


def moe_router(x, rwt, rb, *, tm):
    n, d = x.shape
    return pl.pallas_call(
        _router_kernel,
        grid=(n // tm,),
        in_specs=[pl.BlockSpec((tm, d), lambda i: (i, 0)),
                  pl.BlockSpec((N_EXPERTS, d), lambda i: (0, 0)),
                  pl.BlockSpec((N_EXPERTS, 1), lambda i: (0, 0))],
        out_specs=pl.BlockSpec((N_EXPERTS, tm), lambda i: (0, i)),
        out_shape=jax.ShapeDtypeStruct((N_EXPERTS, n), F32),
        compiler_params=_cparams(("parallel",)),
        name="router",
    )(x, rwt, rb)


def _moe_kernel(xb_ref, x_ref, comb_ref, wg_ref, wu_ref, wd_ref, g_ref, b_ref, o_ref, ob_ref, acc_ref):
    e = pl.program_id(1)

    @pl.when(e == 0)
    def _():
        acc_ref[...] = jnp.zeros_like(acc_ref)

    xb = xb_ref[...]
    hg = _dot(xb, wg_ref[...])
    hu = _dot(xb, wu_ref[...])
    comb = comb_ref[...]
    lane = lax.broadcasted_iota(jnp.int32, comb.shape, 1)
    c = jnp.sum(jnp.where(lane == e, comb, 0.0), axis=1, keepdims=True)
    hid = (hg * jax.nn.sigmoid(hg)) * hu * c
    acc_ref[...] += _dot(hid.astype(BF16), wd_ref[...])

    @pl.when(e == pl.num_programs(1) - 1)
    def _():
        y = _layer_norm_rows(DN_ALPHA * x_ref[...] + acc_ref[...], g_ref[...], b_ref[...])
        o_ref[...] = y
        ob_ref[...] = y.astype(BF16)


def moe_ffn_ln(xb, x, comb, wg, wu, wd, g, b, *, tm):
    n, d = x.shape
    return pl.pallas_call(
        _moe_kernel,
        grid=(n // tm, N_EXPERTS),
        in_specs=[pl.BlockSpec((tm, d), lambda i, e: (i, 0)),
                  pl.BlockSpec((tm, d), lambda i, e: (i, 0)),
                  pl.BlockSpec((tm, N_EXPERTS), lambda i, e: (i, 0)),
                  pl.BlockSpec((None, d, D_FF), lambda i, e: (e, 0, 0)),
                  pl.BlockSpec((None, d, D_FF), lambda i, e: (e, 0, 0)),
                  pl.BlockSpec((None, D_FF, d), lambda i, e: (e, 0, 0)),
                  pl.BlockSpec((1, d), lambda i, e: (0, 0)),
                  pl.BlockSpec((1, d), lambda i, e: (0, 0))],
        out_specs=[pl.BlockSpec((tm, d), lambda i, e: (i, 0)),
                   pl.BlockSpec((tm, d), lambda i, e: (i, 0))],
        out_shape=[jax.ShapeDtypeStruct((n, d), F32), jax.ShapeDtypeStruct((n, d), BF16)],
        scratch_shapes=[pltpu.VMEM((tm, d), F32)],
        compiler_params=_cparams(("parallel", "arbitrary")),
        name="moe",
    )(xb, x, comb, wg, wu, wd, g.reshape(1, d), b.reshape(1, d))


def _rope_tables(pos, head_dim):
    half = head_dim // 2
    inv_freq = ROPE_THETA ** (-jnp.arange(half, dtype=F32) / half)
    ang = pos.astype(F32)[:, None] * inv_freq[None, :]
    cos, sin = jnp.cos(ang), jnp.sin(ang)
    reps = LANE // head_dim
    return (jnp.tile(jnp.concatenate([cos, cos], -1), (1, reps)),
            jnp.tile(jnp.concatenate([-sin, sin], -1), (1, reps)))


def _pack_w_in(w):
    gq, gk, gv, glr, gog, dq, dk, dv, iq, ik, iw = jnp.split(w, SPLIT_POINTS, axis=-1)
    pad = jnp.zeros((w.shape[0], LANE - GATE_RANK - IDX_H), w.dtype)
    wg = jnp.concatenate([gq, gk, gv, gog, dv, glr, iw, pad], -1).astype(BF16)
    wr = jnp.concatenate([dq, dk], -1).astype(BF16)
    wi = jnp.concatenate([iq, ik, ik], -1).astype(BF16)
    return wg, wr, wi


def _pack_wa2(wa2):
    w = wa2.reshape(GATE_RANK, GLA_H, GLA_DK).transpose(1, 0, 2)
    return jnp.pad(w, ((0, 0), (MISC_LR, LANE - GATE_RANK - MISC_LR), (0, 0)))


def kernel(x_prompt, x_sample, cache_dsa_k, cache_dsa_v, cache_idx_k, state_gla, cache_mem_k, cache_mem_v,
           mem_prompt, ln_in_g, ln_in_b, w_in, gla_wa2, gla_ba, gla_norm_g, w_out, ln1_g, ln1_b,
           xa_wq, xa_wk, xa_wv, xa_wo, ln2_g, ln2_b, router_w, router_b,
           moe_w_gate, moe_w_up, moe_w_down, ln3_g, ln3_b):
    tm = 768
    tm_ln = 256
    x, xb = layer_norm_rows([x_prompt.reshape(N_PROMPT, D_MODEL), x_sample.reshape(N_SAMPLE, D_MODEL)],
                            ln_in_g, ln_in_b, tm_ln)

    pos = jnp.concatenate([jnp.tile(jnp.arange(SEQ), BATCH),
                           jnp.tile(PAST_LEN + jnp.arange(DEC_SEQ), DEC_BATCH)])
    rope128 = _rope_tables(pos, DSA_HD) + (DSA_HD // 2,)
    rope64 = _rope_tables(pos, IDX_D) + (IDX_D // 2,)
    memb = mem_prompt.reshape(BATCH * N_MEM, D_MODEL).astype(BF16)
    rwt = router_w.T
    rb = router_b.reshape(N_EXPERTS, 1)

    p_states, s_states = [], []
    for l in range(DEPTH):
        wg, wr, wi = _pack_w_in(w_in[l])
        pg = matmul(xb, wg, tm=tm, tn=1152, name="in_proj_g")
        pr = matmul(xb, wr, tm=tm, tn=WR_COLS, rope=rope128, name="in_proj_r")
        pi = matmul(xb, wi, tm=tm, tn=WI_COLS, rope=rope64, name="in_proj_i")

        wa2p = _pack_wa2(gla_wa2[l])
        ba = gla_ba[l].reshape(GLA_H, 1, GLA_DK)
        gn = gla_norm_g[l].reshape(1, GLA_DV)
        mixg_p, st_p = gla_mix(pg, wa2p, ba, gn, None, nb=BATCH, seq=SEQ, row0=0,
                               tb=256, chunk=CHUNK, sub=GLA_SUB)
        mixg_s, st_s = gla_mix(pg, wa2p, ba, gn, state_gla[l], nb=DEC_BATCH, seq=DEC_SEQ, row0=N_PROMPT,
                               tb=DEC_SEQ, chunk=DEC_SEQ, sub=DEC_SEQ)

        od_p = dsa_prompt(pr, pi, pg, nb=BATCH, seq=SEQ, nq=128, chunk=CHUNK,
                          topk=min(TOPK_MAX, SEQ // 4), step=256)
        od_s = dsa_sample(pr, pi, pg, cache_dsa_k, cache_dsa_v, cache_idx_k, l, nb=DEC_BATCH, nq=DEC_SEQ,
                          row0=N_PROMPT, past=PAST_LEN, topk=min(TOPK_MAX, (PAST_LEN + DEC_SEQ) // 4))

        wo = w_out[l].astype(BF16)
        x1, x1b = matmul_ln([[mixg_p, mixg_s], [od_p, od_s]], [wo[:GLA_H * GLA_DV], wo[GLA_H * GLA_DV:]],
                            x, ln1_g[l], ln1_b[l], tm=tm_ln, name="out_proj_ln1")

        q_xa = matmul(x1b, xa_wq[l].astype(BF16), tm=tm, tn=1024, out_dtype=BF16, name="xa_q")
        mk_p = matmul(memb, xa_wk[l].astype(BF16), tm=512, tn=1024, name="mem_k")
        mv_p = matmul(memb, xa_wv[l].astype(BF16), tm=512, tn=1024, name="mem_v")
        xa_p = cross_attend(q_xa, mk_p, mv_p, lambda b: (b, 0), (N_MEM, D_MODEL),
                            nb=BATCH, seq=SEQ, row0=0, tq=512)
        xa_s = cross_attend(q_xa, cache_mem_k, cache_mem_v, lambda b, l=l: (l, b, 0, 0, 0),
                            (None, None, N_MEM, XA_H, XA_HD), nb=DEC_BATCH, seq=DEC_SEQ, row0=N_PROMPT, tq=DEC_SEQ)
        x2, x2b = matmul_ln([[xa_p, xa_s]], [xa_wo[l].astype(BF16)], x1, ln2_g[l], ln2_b[l],
                            tm=tm_ln, name="xa_o_ln2")

        comb = moe_router(x2, rwt, rb, tm=tm).T
        x, xb = moe_ffn_ln(x2b, x2, comb, moe_w_gate[l].astype(BF16), moe_w_up[l].astype(BF16),
                           moe_w_down[l].astype(BF16), ln3_g[l], ln3_b[l], tm=384)

        kv = DSA_KVH * DSA_HD
        p_states.append((pr[:N_PROMPT, DSA_H * DSA_HD:].reshape(BATCH, SEQ, DSA_KVH, DSA_HD),
                         pg[:N_PROMPT, G_DV:G_DV + kv].reshape(BATCH, SEQ, DSA_KVH, DSA_HD),
                         pi[:N_PROMPT, IDX_H * IDX_D:IDX_H * IDX_D + IDX_D].reshape(BATCH, SEQ, IDX_D),
                         st_p,
                         mk_p.reshape(BATCH, N_MEM, XA_H, XA_HD),
                         mv_p.reshape(BATCH, N_MEM, XA_H, XA_HD)))
        s_states.append((pr[N_PROMPT:, DSA_H * DSA_HD:].reshape(DEC_BATCH, DEC_SEQ, DSA_KVH, DSA_HD),
                         pg[N_PROMPT:, G_DV:G_DV + kv].reshape(DEC_BATCH, DEC_SEQ, DSA_KVH, DSA_HD),
                         pi[N_PROMPT:, IDX_H * IDX_D:IDX_H * IDX_D + IDX_D].reshape(DEC_BATCH, DEC_SEQ, IDX_D),
                         st_s))

    def stack(states, i):
        return jnp.stack([st[i] for st in states])

    return (x[:N_PROMPT].reshape(BATCH, SEQ, D_MODEL), x[N_PROMPT:].reshape(DEC_BATCH, DEC_SEQ, D_MODEL),
            stack(p_states, 0), stack(p_states, 1), stack(p_states, 2), stack(p_states, 3),
            stack(p_states, 4), stack(p_states, 5),
            stack(s_states, 0), stack(s_states, 1), stack(s_states, 2), stack(s_states, 3))
```

```python
import functools

import jax
import jax.numpy as jnp
import numpy as np
from jax import lax
from jax.experimental import pallas as pl
from jax.experimental.pallas import tpu as pltpu

F32 = jnp.float32
BF16 = jnp.bfloat16

D_MODEL = 2048
BATCH = 4
SEQ = 2048
DEPTH = 2
DEC_BATCH = 16
DEC_SEQ = 16
PAST_LEN = 4096
CHUNK = 64
N_MEM = 256
GLA_H = 4
GLA_DK = 128
GLA_DV = 256
GATE_RANK = 16
GATE_TAU = 16.0
DSA_H = 8
DSA_KVH = 2
DSA_HD = 128
IDX_H = 16
IDX_D = 64
TOPK_MAX = 256
XA_H = 4
XA_HD = D_MODEL // XA_H
N_EXPERTS = 16
N_GROUPS = 4
EXP_PER_GROUP = N_EXPERTS // N_GROUPS
D_FF = 512
ROPE_THETA = 10000.0
LN_EPS = 1e-5
DN_ALPHA = (2 * DEPTH) ** 0.25

IN_SPLITS = (GLA_H * GLA_DK, GLA_H * GLA_DK, GLA_H * GLA_DV, GATE_RANK, GLA_H * GLA_DV,
             DSA_H * DSA_HD, DSA_KVH * DSA_HD, DSA_KVH * DSA_HD, IDX_H * IDX_D, IDX_D, IDX_H)
SPLIT_POINTS = tuple(int(c) for c in np.cumsum(IN_SPLITS)[:-1])

N_PROMPT = BATCH * SEQ
N_SAMPLE = DEC_BATCH * DEC_SEQ
N_TOK = N_PROMPT + N_SAMPLE

LANE = 128
GLA_SUB = 16
VMEM_LIMIT = 48 * 1024 * 1024

G_Q, G_K, G_V, G_OG, G_DV, G_MISC = 0, 512, 1024, 2048, 3072, 3328
WG_COLS = 3456
WR_COLS = 1280
WI_COLS = 1152
MISC_LR = 0
MISC_IW = GATE_RANK


def _cparams(sem):
    return pltpu.CompilerParams(dimension_semantics=sem, vmem_limit_bytes=VMEM_LIMIT)


def _dot(a, b):
    return jnp.dot(a, b, preferred_element_type=F32)


def _dot_nt(a, b):
    return lax.dot_general(a, b, (((1,), (1,)), ((), ())), preferred_element_type=F32)


def _dot_tn(a, b):
    return lax.dot_general(a, b, (((0,), (0,)), ((), ())), preferred_element_type=F32)


def _layer_norm_rows(y, g, b):
    mu = jnp.mean(y, axis=-1, keepdims=True)
    d = y - mu
    var = jnp.mean(d * d, axis=-1, keepdims=True)
    return d * lax.rsqrt(var + LN_EPS) * g + b


def _seg_specs(segs, tm):
    specs, starts, start = [], [], 0
    for a in segs:
        nt = a.shape[0] // tm
        specs.append(pl.BlockSpec((tm, a.shape[1]),
                                  functools.partial(lambda i, s0, n: (jnp.clip(i - s0, 0, n - 1), 0), s0=start, n=nt)))
        starts.append(start)
        start += nt
    return specs, tuple(starts), start


def _seg_load(refs, starts):
    i = pl.program_id(0)
    v = refs[0][...]
    for r, s0 in zip(refs[1:], starts[1:]):
        v = jnp.where(i >= s0, r[...], v)
    return v


def _ln_kernel(*refs, starts):
    n = len(starts)
    g_ref, b_ref, o_ref, ob_ref = refs[n:]
    y = _layer_norm_rows(_seg_load(refs[:n], starts), g_ref[...], b_ref[...])
    o_ref[...] = y
    ob_ref[...] = y.astype(BF16)


def layer_norm_rows(segs, g, b, tm):
    d = segs[0].shape[1]
    specs, starts, nt = _seg_specs(segs, tm)
    return pl.pallas_call(
        functools.partial(_ln_kernel, starts=starts),
        grid=(nt,),
        in_specs=specs + [pl.BlockSpec((1, d), lambda i: (0, 0)),
                          pl.BlockSpec((1, d), lambda i: (0, 0))],
        out_specs=[pl.BlockSpec((tm, d), lambda i: (i, 0)),
                   pl.BlockSpec((tm, d), lambda i: (i, 0))],
        out_shape=[jax.ShapeDtypeStruct((nt * tm, d), F32), jax.ShapeDtypeStruct((nt * tm, d), BF16)],
        compiler_params=_cparams(("parallel",)),
        name="ln_in",
    )(*segs, g.reshape(1, d), b.reshape(1, d))


def _swap_halves(x, half):
    if half == LANE // 2:
        return pltpu.roll(x, half, 1)
    lane = lax.broadcasted_iota(jnp.int32, x.shape, 1)
    lower = (lane % (2 * half)) < half
    return jnp.where(lower, pltpu.roll(x, LANE - half, 1), pltpu.roll(x, half, 1))


def _mm_kernel(a_ref, w_ref, *rest, rope_half, tn):
    acc = _dot(a_ref[...], w_ref[...])
    if rope_half is None:
        (o_ref,) = rest
        o_ref[...] = acc.astype(o_ref.dtype)
        return
    cos_ref, sin_ref, o_ref = rest
    cos = cos_ref[...]
    sin = sin_ref[...]
    for c in range(tn // LANE):
        x = acc[:, c * LANE:(c + 1) * LANE]
        o_ref[:, c * LANE:(c + 1) * LANE] = (x * cos + _swap_halves(x, rope_half) * sin).astype(o_ref.dtype)


def matmul(a, w, *, tm, tn, out_dtype=F32, rope=None, name="mm"):
    m, k = a.shape
    n = w.shape[1]
    in_specs = [pl.BlockSpec((tm, k), lambda j, i: (i, 0)),
                pl.BlockSpec((k, tn), lambda j, i: (0, j))]
    args = [a, w]
    rope_half = None
    if rope is not None:
        cos, sin, rope_half = rope
        in_specs += [pl.BlockSpec((tm, LANE), lambda j, i: (i, 0)),
                     pl.BlockSpec((tm, LANE), lambda j, i: (i, 0))]
        args += [cos, sin]
    return pl.pallas_call(
        functools.partial(_mm_kernel, rope_half=rope_half, tn=tn),
        grid=(n // tn, m // tm),
        in_specs=in_specs,
        out_specs=pl.BlockSpec((tm, tn), lambda j, i: (i, j)),
        out_shape=jax.ShapeDtypeStruct((m, n), out_dtype),
        compiler_params=_cparams(("parallel", "parallel")),
        name=name,
    )(*args)


def _mm_ln_kernel(*refs, seg_starts):
    pos = 0
    a_vals = []
    for starts in seg_starts:
        a_vals.append(_seg_load(refs[pos:pos + len(starts)], starts))
        pos += len(starts)
    n_pairs = len(seg_starts)
    w_refs = refs[pos:pos + n_pairs]
    resid_ref, g_ref, b_ref, o_ref, ob_ref = refs[pos + n_pairs:]
    acc = _dot(a_vals[0], w_refs[0][...])
    for a, w_ref in zip(a_vals[1:], w_refs[1:]):
        acc = acc + _dot(a, w_ref[...])
    y = _layer_norm_rows(DN_ALPHA * resid_ref[...] + acc, g_ref[...], b_ref[...])
    o_ref[...] = y
    ob_ref[...] = y.astype(BF16)


def matmul_ln(a_list, w_list, resid, g, b, *, tm, name):
    m, d = resid.shape
    a_specs, seg_starts, a_args = [], [], []
    for segs in a_list:
        specs, starts, nt = _seg_specs(segs, tm)
        assert nt * tm == m
        a_specs += specs
        seg_starts.append(starts)
        a_args += list(segs)
    in_specs = (a_specs
                + [pl.BlockSpec(w.shape, lambda i: (0, 0)) for w in w_list]
                + [pl.BlockSpec((tm, d), lambda i: (i, 0)),
                   pl.BlockSpec((1, d), lambda i: (0, 0)),
                   pl.BlockSpec((1, d), lambda i: (0, 0))])
    return pl.pallas_call(
        functools.partial(_mm_ln_kernel, seg_starts=tuple(seg_starts)),
        grid=(m // tm,),
        in_specs=in_specs,
        out_specs=[pl.BlockSpec((tm, d), lambda i: (i, 0)),
                   pl.BlockSpec((tm, d), lambda i: (i, 0))],
        out_shape=[jax.ShapeDtypeStruct((m, d), F32), jax.ShapeDtypeStruct((m, d), BF16)],
        compiler_params=_cparams(("parallel",)),
        name=name,
    )(*a_args, *w_list, resid, g.reshape(1, d), b.reshape(1, d))


def _log_sigmoid(z):
    return jnp.minimum(z, 0.0) - jnp.log1p(jnp.exp(-jnp.abs(z)))


def _gla_kernel(*refs, tb, chunk, sub, has_state):
    if has_state:
        q_ref, k_ref, v_ref, og_ref, misc_ref, wa2_ref, ba_ref, gn_ref, s0_ref, o_ref, sout_ref, st_ref = refs
    else:
        q_ref, k_ref, v_ref, og_ref, misc_ref, wa2_ref, ba_ref, gn_ref, o_ref, sout_ref, st_ref = refs
    t = pl.program_id(2)

    @pl.when(t == 0)
    def _():
        if has_state:
            st_ref[...] = s0_ref[...].T
        else:
            st_ref[...] = jnp.zeros_like(st_ref)

    z = jnp.dot(misc_ref[...], wa2_ref[...], preferred_element_type=F32,
                precision=lax.Precision.HIGHEST) + ba_ref[...]
    g_all = _log_sigmoid(z) * (1.0 / GATE_TAU)

    row = lax.broadcasted_iota(jnp.int32, (chunk, chunk), 0)
    col = lax.broadcasted_iota(jnp.int32, (chunk, chunk), 1)
    tril = (row >= col).astype(F32)
    gn = gn_ref[...]

    for c in range(tb // chunk):
        rows = slice(c * chunk, (c + 1) * chunk)
        G = jnp.dot(tril, g_all[rows], preferred_element_type=F32,
                    precision=lax.Precision.HIGHEST)
        qc = q_ref[rows, :] * (GLA_DK ** -0.5)
        kc = k_ref[rows, :]
        vb = v_ref[rows, :].astype(BF16)
        st = st_ref[...]
        o = _dot_nt((qc * jnp.exp(G)).astype(BF16), st.astype(BF16))
        o_parts = []
        for i in range(chunk // sub):
            r0 = i * sub
            nk = r0 + sub
            b_i = G[r0:r0 + 1, :]
            qt = qc[r0:nk] * jnp.exp(G[r0:nk] - b_i)
            kt = kc[:nk] * jnp.exp(b_i - G[:nk])
            a = _dot_nt(qt.astype(BF16), kt.astype(BF16))
            t_idx = r0 + lax.broadcasted_iota(jnp.int32, (sub, nk), 0)
            s_idx = lax.broadcasted_iota(jnp.int32, (sub, nk), 1)
            a = jnp.where(t_idx >= s_idx, a, 0.0)
            o_parts.append(_dot(a.astype(BF16), vb[:nk]))
        o = o + jnp.concatenate(o_parts, axis=0)
        g_last = G[chunk - 1:chunk, :]
        k_dec = kc * jnp.exp(g_last - G)
        st_ref[...] = st * jnp.exp(g_last) + _dot_tn(vb, k_dec.astype(BF16))
        n = o * lax.rsqrt(jnp.mean(o * o, axis=-1, keepdims=True) + LN_EPS) * gn
        og = og_ref[rows, :]
        o_ref[rows, :] = (n * (og * jax.nn.sigmoid(og))).astype(o_ref.dtype)

    @pl.when(t == pl.num_programs(2) - 1)
    def _():
        sout_ref[...] = st_ref[...].T


def gla_mix(pg, wa2p, ba, gn, s0, *, nb, seq, row0, tb, chunk, sub):
    nt = seq // tb
    rb0 = row0 // tb

    def tok(colblock):
        return lambda b, h, t: (rb0 + b * nt + t, colblock(h))

    in_specs = [
        pl.BlockSpec((tb, GLA_DK), tok(lambda h: G_Q // GLA_DK + h)),
        pl.BlockSpec((tb, GLA_DK), tok(lambda h: G_K // GLA_DK + h)),
        pl.BlockSpec((tb, GLA_DV), tok(lambda h: G_V // GLA_DV + h)),
        pl.BlockSpec((tb, GLA_DV), tok(lambda h: G_OG // GLA_DV + h)),
        pl.BlockSpec((tb, LANE), tok(lambda h: G_MISC // LANE)),
        pl.BlockSpec((None, LANE, GLA_DK), lambda b, h, t: (h, 0, 0)),
        pl.BlockSpec((None, 1, GLA_DK), lambda b, h, t: (h, 0, 0)),
        pl.BlockSpec((1, GLA_DV), lambda b, h, t: (0, 0)),
    ]
    args = [pg, pg, pg, pg, pg, wa2p, ba, gn]
    if s0 is not None:
        in_specs.append(pl.BlockSpec((None, None, GLA_DK, GLA_DV), lambda b, h, t: (b, h, 0, 0)))
        args.append(s0)
    return pl.pallas_call(
        functools.partial(_gla_kernel, tb=tb, chunk=chunk, sub=sub, has_state=s0 is not None),
        grid=(nb, GLA_H, nt),
        in_specs=in_specs,
        out_specs=[pl.BlockSpec((tb, GLA_DV), lambda b, h, t: (b * nt + t, h)),
                   pl.BlockSpec((None, None, GLA_DK, GLA_DV), lambda b, h, t: (b, h, 0, 0))],
        out_shape=[jax.ShapeDtypeStruct((nb * seq, GLA_H * GLA_DV), BF16),
                   jax.ShapeDtypeStruct((nb, GLA_H, GLA_DK, GLA_DV), F32)],
        scratch_shapes=[pltpu.VMEM((GLA_DV, GLA_DK), F32)],
        compiler_params=_cparams(("parallel", "parallel", "arbitrary")),
        name="gla",
    )(*args)


INT_MIN = -2 ** 31
NEG_INF = float("-inf")


def _order_key(score):
    bits = pltpu.bitcast(score, jnp.int32)
    return jnp.where(bits < 0, bits ^ jnp.int32(0x7FFFFFFF), bits)


def _count(mask):
    return jnp.sum(jnp.where(mask, 1.0, 0.0), axis=1, keepdims=True)


def _topk_bias(score, adm, kidx, key_ref, topk):
    nq, n = score.shape
    idx_bits = int(np.ceil(np.log2(n + 1)))
    score = jnp.where(adm, score, NEG_INF)
    key_ref[...] = _order_key(score)

    def thr_body(i, t):
        cand = t + lax.shift_left(jnp.int32(1), 31 - i)
        return jnp.where(_count(key_ref[...] >= cand) >= float(topk), cand, t)

    thr = lax.fori_loop(0, 32, thr_body, jnp.full((nq, 1), INT_MIN, jnp.int32))

    key = key_ref[...]
    above = key > thr
    tied = key == thr
    need = float(topk) - _count(above)
    key_neg_inf = jnp.int32(np.array(-np.inf, np.float32).view(np.int32) ^ 0x7FFFFFFF)
    tie_rows = (_count(tied) > need) & (thr > key_neg_inf)
    any_tie = jnp.max(jnp.where(tie_rows, 1.0, 0.0)) > 0.0

    def tie_cut():
        def cut_body(i, j):
            cand = j + lax.shift_left(jnp.int32(1), idx_bits - 1 - i)
            return jnp.where(_count((key_ref[...] == thr) & (kidx < cand)) <= need, cand, j)
        return lax.fori_loop(0, idx_bits, cut_body, jnp.zeros((nq, 1), jnp.int32))

    cut = lax.cond(any_tie, tie_cut, lambda: jnp.full((nq, 1), 2 ** idx_bits, jnp.int32))
    sel = (above | (tied & (kidx < cut))) & adm & (score < float("inf"))
    return jnp.where(sel, 0.0, NEG_INF)


def _idx_weight(misc, h):
    return misc[:, MISC_IW + h:MISC_IW + h + 1] * (IDX_H ** -0.5 * IDX_D ** -0.5)


def _softmax_pv(s, vb):
    m = jnp.max(s, axis=1, keepdims=True)
    p = jnp.exp(s - m)
    l = jnp.sum(p, axis=1, keepdims=True)
    return _dot(p.astype(BF16), vb) / l


def _dsa_prompt_kernel(q_ref, qi_ref, misc_ref, k_ref, v_ref, ki_ref, o_ref, key_ref,
                       *, nq, seq, chunk, topk, step):
    qb = pl.program_id(1)
    lane = lax.broadcasted_iota(jnp.int32, (nq, LANE), 1)
    upper = lane >= IDX_D
    misc = misc_ref[...]
    groups = DSA_H // DSA_KVH

    def body(n):
        kib = ki_ref[pl.ds(0, n), :].astype(BF16)
        score = jnp.zeros((nq, n), F32)
        for h in range(IDX_H):
            pair = qi_ref[:, (h // 2) * LANE:(h // 2 + 1) * LANE]
            qm = jnp.where(upper == (h % 2 == 1), pair, 0.0).astype(BF16)
            score = score + jnp.maximum(_dot_nt(qm, kib), 0.0) * _idx_weight(misc, h)
        kidx = lax.broadcasted_iota(jnp.int32, (nq, n), 1)
        qpos = qb * nq + lax.broadcasted_iota(jnp.int32, (nq, 1), 0)
        limit = (lax.shift_right_logical(qpos, int(np.log2(chunk))) + 1) * chunk
        bias = _topk_bias(score, kidx < limit, kidx, key_ref.at[:, pl.ds(0, n)], topk)
        for j in range(DSA_KVH):
            kb = k_ref[pl.ds(0, n), j * DSA_HD:(j + 1) * DSA_HD].astype(BF16)
            vb = v_ref[pl.ds(0, n), j * DSA_HD:(j + 1) * DSA_HD].astype(BF16)
            for g in range(groups):
                h = j * groups + g
                qh = q_ref[:, h * DSA_HD:(h + 1) * DSA_HD].astype(BF16)
                s = _dot_nt(qh, kb) * (DSA_HD ** -0.5) + bias
                o_ref[:, h * DSA_HD:(h + 1) * DSA_HD] = _softmax_pv(s, vb).astype(o_ref.dtype)

    n_need = (qb + 1) * nq
    cls = (n_need + step - 1) // step - 1
    for c in range(seq // step):
        pl.when(cls == c)(functools.partial(body, (c + 1) * step))


def dsa_prompt(pr, pi, pg, *, nb, seq, nq, chunk, topk, step):
    nqb = seq // nq
    kw = DSA_KVH * DSA_HD
    qmap = lambda cb: (lambda b, t: (b * nqb + t, cb))
    return pl.pallas_call(
        functools.partial(_dsa_prompt_kernel, nq=nq, seq=seq, chunk=chunk, topk=topk, step=step),
        grid=(nb, nqb),
        in_specs=[pl.BlockSpec((nq, DSA_H * DSA_HD), qmap(0)),
                  pl.BlockSpec((nq, IDX_H * IDX_D), qmap(0)),
                  pl.BlockSpec((nq, LANE), qmap(G_MISC // LANE)),
                  pl.BlockSpec((seq, kw), lambda b, t: (b, DSA_H * DSA_HD // kw)),
                  pl.BlockSpec((seq, kw), lambda b, t: (b, G_DV // kw)),
                  pl.BlockSpec((seq, LANE), lambda b, t: (b, IDX_H * IDX_D // LANE))],
        out_specs=pl.BlockSpec((nq, DSA_H * DSA_HD), lambda b, t: (b * nqb + t, 0)),
        out_shape=jax.ShapeDtypeStruct((nb * seq, DSA_H * DSA_HD), BF16),
        scratch_shapes=[pltpu.VMEM((nq, seq), jnp.int32)],
        compiler_params=_cparams(("parallel", "arbitrary")),
        name="dsa_prompt",
    )(pr, pi, pg, pr, pg, pi)


def _dsa_sample_kernel(q_ref, qi_ref, misc_ref, kn_ref, vn_ref, kin_ref, ck_ref, cv_ref, cik_ref, o_ref, key_ref,
                       *, nq, past, topk):
    misc = misc_ref[...]
    groups = DSA_H // DSA_KVH
    zrows = LANE - nq
    cikb = cik_ref[...].astype(BF16)
    kinb = jnp.concatenate([kin_ref[:, :IDX_D], jnp.zeros((zrows, IDX_D), F32)], 0).astype(BF16)
    qs = jnp.concatenate([qi_ref[:, h * IDX_D:(h + 1) * IDX_D] for h in range(IDX_H)], 0).astype(BF16)
    lg_c = _dot_nt(qs, cikb)
    lg_n = _dot_nt(qs, kinb)
    sc_c = jnp.zeros((nq, past), F32)
    sc_n = jnp.zeros((nq, LANE), F32)
    for h in range(IDX_H):
        w = _idx_weight(misc, h)
        sc_c = sc_c + jnp.maximum(lg_c[h * nq:(h + 1) * nq], 0.0) * w
        sc_n = sc_n + jnp.maximum(lg_n[h * nq:(h + 1) * nq], 0.0) * w
    n = past + LANE
    score = jnp.concatenate([sc_c, sc_n], 1)
    kidx = lax.broadcasted_iota(jnp.int32, (nq, n), 1)
    bias = _topk_bias(score, kidx < past + nq, kidx, key_ref, topk)
    bias_g = jnp.concatenate([bias] * groups, 0)
    for j in range(DSA_KVH):
        cols = slice(j * DSA_HD, (j + 1) * DSA_HD)
        kb_c = ck_ref[:, j, :].astype(BF16)
        vb_c = cv_ref[:, j, :].astype(BF16)
        kb_n = jnp.concatenate([kn_ref[:, cols], jnp.zeros((zrows, DSA_HD), F32)], 0).astype(BF16)
        vb_n = jnp.concatenate([vn_ref[:, cols], jnp.zeros((zrows, DSA_HD), F32)], 0).astype(BF16)
        qg = jnp.concatenate([q_ref[:, (j * groups + g) * DSA_HD:(j * groups + g + 1) * DSA_HD]
                              for g in range(groups)], 0).astype(BF16)
        s = jnp.concatenate([_dot_nt(qg, kb_c), _dot_nt(qg, kb_n)], 1) * (DSA_HD ** -0.5) + bias_g
        m = jnp.max(s, axis=1, keepdims=True)
        p = jnp.exp(s - m)
        l = jnp.sum(p, axis=1, keepdims=True)
        pb = p.astype(BF16)
        o = (_dot(pb[:, :past], vb_c) + _dot(pb[:, past:], vb_n)) / l
        for g in range(groups):
            h = j * groups + g
            o_ref[:, h * DSA_HD:(h + 1) * DSA_HD] = o[g * nq:(g + 1) * nq].astype(o_ref.dtype)


def dsa_sample(pr, pi, pg, cache_k, cache_v, cache_ik, layer, *, nb, nq, row0, past, topk):
    rb0 = row0 // nq
    kw = DSA_KVH * DSA_HD
    smap = lambda cb: (lambda b: (rb0 + b, cb))
    return pl.pallas_call(
        functools.partial(_dsa_sample_kernel, nq=nq, past=past, topk=topk),
        grid=(nb,),
        in_specs=[pl.BlockSpec((nq, DSA_H * DSA_HD), smap(0)),
                  pl.BlockSpec((nq, IDX_H * IDX_D), smap(0)),
                  pl.BlockSpec((nq, LANE), smap(G_MISC // LANE)),
                  pl.BlockSpec((nq, kw), smap(DSA_H * DSA_HD // kw)),
                  pl.BlockSpec((nq, kw), smap(G_DV // kw)),
                  pl.BlockSpec((nq, LANE), smap(IDX_H * IDX_D // LANE)),
                  pl.BlockSpec((None, None, past, DSA_KVH, DSA_HD), lambda b: (layer, b, 0, 0, 0)),
                  pl.BlockSpec((None, None, past, DSA_KVH, DSA_HD), lambda b: (layer, b, 0, 0, 0)),
                  pl.BlockSpec((None, None, past, IDX_D), lambda b: (layer, b, 0, 0))],
        out_specs=pl.BlockSpec((nq, DSA_H * DSA_HD), lambda b: (b, 0)),
        out_shape=jax.ShapeDtypeStruct((nb * nq, DSA_H * DSA_HD), BF16),
        scratch_shapes=[pltpu.VMEM((nq, past + LANE), jnp.int32)],
        compiler_params=_cparams(("parallel",)),
        name="dsa_sample",
    )(pr, pi, pg, pr, pg, pi, cache_k, cache_v, cache_ik)


def _xattn_kernel(q_ref, mk_ref, mv_ref, o_ref, *, heads_split):
    for h in range(XA_H):
        cols = slice(h * XA_HD, (h + 1) * XA_HD)
        mk = mk_ref[:, h, :] if heads_split else mk_ref[:, cols]
        mv = mv_ref[:, h, :] if heads_split else mv_ref[:, cols]
        s = _dot_nt(q_ref[:, cols], mk.astype(BF16)) * (XA_HD ** -0.5)
        o_ref[:, cols] = _softmax_pv(s, mv.astype(BF16)).astype(o_ref.dtype)


def cross_attend(q, mk, mv, mem_index, mem_block, *, nb, seq, row0, tq):
    nt = seq // tq
    rb0 = row0 // tq
    d = q.shape[1]
    mem_spec = pl.BlockSpec(mem_block, lambda b, t: mem_index(b))
    return pl.pallas_call(
        functools.partial(_xattn_kernel, heads_split=len([s for s in mem_block if s is not None]) == 3),
        grid=(nb, nt),
        in_specs=[pl.BlockSpec((tq, d), lambda b, t: (rb0 + b * nt + t, 0)), mem_spec, mem_spec],
        out_specs=pl.BlockSpec((tq, d), lambda b, t: (b * nt + t, 0)),
        out_shape=jax.ShapeDtypeStruct((nb * seq, d), BF16),
        compiler_params=_cparams(("parallel", "parallel")),
        name="xattn",
    )(q, mk, mv)


def _top2_sum(vals):
    a, b, c, d = vals
    m1, n1 = jnp.maximum(a, b), jnp.minimum(a, b)
    m2, n2 = jnp.maximum(c, d), jnp.minimum(c, d)
    return jnp.maximum(m1, m2) + jnp.maximum(jnp.minimum(m1, m2), jnp.maximum(n1, n2))


def _router_kernel(x_ref, rwt_ref, rb_ref, comb_ref):
    logits = lax.dot_general(rwt_ref[...], x_ref[...], (((1,), (1,)), ((), ())),
                             preferred_element_type=F32, precision=lax.Precision.HIGHEST)
    aff = jax.nn.sigmoid(logits)
    biased = aff + rb_ref[...]
    a = [aff[e:e + 1, :] for e in range(N_EXPERTS)]
    b = [biased[e:e + 1, :] for e in range(N_EXPERTS)]
    gs = [_top2_sum(b[g * EXP_PER_GROUP:(g + 1) * EXP_PER_GROUP]) for g in range(N_GROUPS)]
    one = jnp.ones_like(a[0])
    zero = jnp.zeros_like(a[0])
    gated = []
    for g in range(N_GROUPS):
        win = one
        for g2 in range(N_GROUPS):
            if g2 < g:
                win = win * jnp.where(gs[g] > gs[g2], one, zero)
            elif g2 > g:
                win = win * jnp.where(gs[g] >= gs[g2], one, zero)
        for e in range(g * EXP_PER_GROUP, (g + 1) * EXP_PER_GROUP):
            beaten = zero
            for f in range(g * EXP_PER_GROUP, (g + 1) * EXP_PER_GROUP):
                if f < e:
                    beaten = beaten + jnp.where(b[f] >= b[e], one, zero)
                elif f > e:
                    beaten = beaten + jnp.where(b[f] > b[e], one, zero)
            gated.append(jnp.where(beaten < 2.0, win, zero) * a[e])
    denom = gated[0]
    for t in gated[1:]:
        denom = denom + t
    comb_ref[...] = jnp.concatenate(gated, axis=0) / denom


def moe_router(x, rwt, rb, *, tm):
    n, d = x.shape
    return pl.pallas_call(
        _router_kernel,
        grid=(n // tm,),
        in_specs=[pl.BlockSpec((tm, d), lambda i: (i, 0)),
                  pl.BlockSpec((N_EXPERTS, d), lambda i: (0, 0)),
                  pl.BlockSpec((N_EXPERTS, 1), lambda i: (0, 0))],
        out_specs=pl.BlockSpec((N_EXPERTS, tm), lambda i: (0, i)),
        out_shape=jax.ShapeDtypeStruct((N_EXPERTS, n), F32),
        compiler_params=_cparams(("parallel",)),
        name="router",
    )(x, rwt, rb)


def _moe_kernel(xb_ref, x_ref, comb_ref, wg_ref, wu_ref, wd_ref, g_ref, b_ref, o_ref, ob_ref, acc_ref):
    e = pl.program_id(1)

    @pl.when(e == 0)
    def _():
        acc_ref[...] = jnp.zeros_like(acc_ref)

    xb = xb_ref[...]
    hg = _dot(xb, wg_ref[...])
    hu = _dot(xb, wu_ref[...])
    comb = comb_ref[...]
    lane = lax.broadcasted_iota(jnp.int32, comb.shape, 1)
    c = jnp.sum(jnp.where(lane == e, comb, 0.0), axis=1, keepdims=True)
    hid = (hg * jax.nn.sigmoid(hg)) * hu * c
    acc_ref[...] += _dot(hid.astype(BF16), wd_ref[...])

    @pl.when(e == pl.num_programs(1) - 1)
    def _():
        y = _layer_norm_rows(DN_ALPHA * x_ref[...] + acc_ref[...], g_ref[...], b_ref[...])
        o_ref[...] = y
        ob_ref[...] = y.astype(BF16)


def moe_ffn_ln(xb, x, comb, wg, wu, wd, g, b, *, tm):
    n, d = x.shape
    return pl.pallas_call(
        _moe_kernel,
        grid=(n // tm, N_EXPERTS),
        in_specs=[pl.BlockSpec((tm, d), lambda i, e: (i, 0)),
                  pl.BlockSpec((tm, d), lambda i, e: (i, 0)),
                  pl.BlockSpec((tm, N_EXPERTS), lambda i, e: (i, 0)),
                  pl.BlockSpec((None, d, D_FF), lambda i, e: (e, 0, 0)),
                  pl.BlockSpec((None, d, D_FF), lambda i, e: (e, 0, 0)),
                  pl.BlockSpec((None, D_FF, d), lambda i, e: (e, 0, 0)),
                  pl.BlockSpec((1, d), lambda i, e: (0, 0)),
                  pl.BlockSpec((1, d), lambda i, e: (0, 0))],
        out_specs=[pl.BlockSpec((tm, d), lambda i, e: (i, 0)),
                   pl.BlockSpec((tm, d), lambda i, e: (i, 0))],
        out_shape=[jax.ShapeDtypeStruct((n, d), F32), jax.ShapeDtypeStruct((n, d), BF16)],
        scratch_shapes=[pltpu.VMEM((tm, d), F32)],
        compiler_params=_cparams(("parallel", "arbitrary")),
        name="moe",
    )(xb, x, comb, wg, wu, wd, g.reshape(1, d), b.reshape(1, d))


def _rope_tables(pos, head_dim):
    half = head_dim // 2
    inv_freq = ROPE_THETA ** (-jnp.arange(half, dtype=F32) / half)
    ang = pos.astype(F32)[:, None] * inv_freq[None, :]
    cos, sin = jnp.cos(ang), jnp.sin(ang)
    reps = LANE // head_dim
    return (jnp.tile(jnp.concatenate([cos, cos], -1), (1, reps)),
            jnp.tile(jnp.concatenate([-sin, sin], -1), (1, reps)))


def _pack_w_in(w):
    gq, gk, gv, glr, gog, dq, dk, dv, iq, ik, iw = jnp.split(w, SPLIT_POINTS, axis=-1)
    pad = jnp.zeros((w.shape[0], LANE - GATE_RANK - IDX_H), w.dtype)
    wg = jnp.concatenate([gq, gk, gv, gog, dv, glr, iw, pad], -1).astype(BF16)
    wr = jnp.concatenate([dq, dk], -1).astype(BF16)
    wi = jnp.concatenate([iq, ik, ik], -1).astype(BF16)
    return wg, wr, wi


def _pack_wa2(wa2):
    w = wa2.reshape(GATE_RANK, GLA_H, GLA_DK).transpose(1, 0, 2)
    return jnp.pad(w, ((0, 0), (MISC_LR, LANE - GATE_RANK - MISC_LR), (0, 0)))


def kernel(x_prompt, x_sample, cache_dsa_k, cache_dsa_v, cache_idx_k, state_gla, cache_mem_k, cache_mem_v,
           mem_prompt, ln_in_g, ln_in_b, w_in, gla_wa2, gla_ba, gla_norm_g, w_out, ln1_g, ln1_b,
           xa_wq, xa_wk, xa_wv, xa_wo, ln2_g, ln2_b, router_w, router_b,
           moe_w_gate, moe_w_up, moe_w_down, ln3_g, ln3_b):
    tm = 768
    tm_ln = 256
    x, xb = layer_norm_rows([x_prompt.reshape(N_PROMPT, D_MODEL), x_sample.reshape(N_SAMPLE, D_MODEL)],
                            ln_in_g, ln_in_b, tm_ln)

    pos = jnp.concatenate([jnp.tile(jnp.arange(SEQ), BATCH),
                           jnp.tile(PAST_LEN + jnp.arange(DEC_SEQ), DEC_BATCH)])
    rope128 = _rope_tables(pos, DSA_HD) + (DSA_HD // 2,)
    rope64 = _rope_tables(pos, IDX_D) + (IDX_D // 2,)
    memb = mem_prompt.reshape(BATCH * N_MEM, D_MODEL).astype(BF16)
    rwt = router_w.T
    rb = router_b.reshape(N_EXPERTS, 1)

    p_states, s_states = [], []
    for l in range(DEPTH):
        wg, wr, wi = _pack_w_in(w_in[l])
        pg = matmul(xb, wg, tm=tm, tn=1152, name="in_proj_g")
        pr = matmul(xb, wr, tm=tm, tn=WR_COLS, rope=rope128, name="in_proj_r")
        pi = matmul(xb, wi, tm=tm, tn=WI_COLS, rope=rope64, name="in_proj_i")

        wa2p = _pack_wa2(gla_wa2[l])
        ba = gla_ba[l].reshape(GLA_H, 1, GLA_DK)
        gn = gla_norm_g[l].reshape(1, GLA_DV)
        mixg_p, st_p = gla_mix(pg, wa2p, ba, gn, None, nb=BATCH, seq=SEQ, row0=0,
                               tb=256, chunk=CHUNK, sub=GLA_SUB)
        mixg_s, st_s = gla_mix(pg, wa2p, ba, gn, state_gla[l], nb=DEC_BATCH, seq=DEC_SEQ, row0=N_PROMPT,
                               tb=DEC_SEQ, chunk=DEC_SEQ, sub=DEC_SEQ)

        od_p = dsa_prompt(pr, pi, pg, nb=BATCH, seq=SEQ, nq=128, chunk=CHUNK,
                          topk=min(TOPK_MAX, SEQ // 4), step=1024)
        od_s = dsa_sample(pr, pi, pg, cache_dsa_k, cache_dsa_v, cache_idx_k, l, nb=DEC_BATCH, nq=DEC_SEQ,
                          row0=N_PROMPT, past=PAST_LEN, topk=min(TOPK_MAX, (PAST_LEN + DEC_SEQ) // 4))

        wo = w_out[l].astype(BF16)
        x1, x1b = matmul_ln([[mixg_p, mixg_s], [od_p, od_s]], [wo[:GLA_H * GLA_DV], wo[GLA_H * GLA_DV:]],
                            x, ln1_g[l], ln1_b[l], tm=tm_ln, name="out_proj_ln1")

        q_xa = matmul(x1b, xa_wq[l].astype(BF16), tm=tm, tn=1024, out_dtype=BF16, name="xa_q")
        mk_p = matmul(memb, xa_wk[l].astype(BF16), tm=512, tn=1024, name="mem_k")
        mv_p = matmul(memb, xa_wv[l].astype(BF16), tm=512, tn=1024, name="mem_v")
        xa_p = cross_attend(q_xa, mk_p, mv_p, lambda b: (b, 0), (N_MEM, D_MODEL),
                            nb=BATCH, seq=SEQ, row0=0, tq=512)
        xa_s = cross_attend(q_xa, cache_mem_k, cache_mem_v, lambda b, l=l: (l, b, 0, 0, 0),
                            (None, None, N_MEM, XA_H, XA_HD), nb=DEC_BATCH, seq=DEC_SEQ, row0=N_PROMPT, tq=DEC_SEQ)
        x2, x2b = matmul_ln([[xa_p, xa_s]], [xa_wo[l].astype(BF16)], x1, ln2_g[l], ln2_b[l],
                            tm=tm_ln, name="xa_o_ln2")

        comb = moe_router(x2, rwt, rb, tm=tm).T
        x, xb = moe_ffn_ln(x2b, x2, comb, moe_w_gate[l].astype(BF16), moe_w_up[l].astype(BF16),
                           moe_w_down[l].astype(BF16), ln3_g[l], ln3_b[l], tm=384)

        kv = DSA_KVH * DSA_HD
        p_states.append((pr[:N_PROMPT, DSA_H * DSA_HD:].reshape(BATCH, SEQ, DSA_KVH, DSA_HD),
                         pg[:N_PROMPT, G_DV:G_DV + kv].reshape(BATCH, SEQ, DSA_KVH, DSA_HD),
                         pi[:N_PROMPT, IDX_H * IDX_D:IDX_H * IDX_D + IDX_D].reshape(BATCH, SEQ, IDX_D),
                         st_p,
                         mk_p.reshape(BATCH, N_MEM, XA_H, XA_HD),
                         mv_p.reshape(BATCH, N_MEM, XA_H, XA_HD)))
        s_states.append((pr[N_PROMPT:, DSA_H * DSA_HD:].reshape(DEC_BATCH, DEC_SEQ, DSA_KVH, DSA_HD),
                         pg[N_PROMPT:, G_DV:G_DV + kv].reshape(DEC_BATCH, DEC_SEQ, DSA_KVH, DSA_HD),
                         pi[N_PROMPT:, IDX_H * IDX_D:IDX_H * IDX_D + IDX_D].reshape(DEC_BATCH, DEC_SEQ, IDX_D),
                         st_s))

    def stack(states, i):
        return jnp.stack([st[i] for st in states])

    return (x[:N_PROMPT].reshape(BATCH, SEQ, D_MODEL), x[N_PROMPT:].reshape(DEC_BATCH, DEC_SEQ, D_MODEL),
            stack(p_states, 0), stack(p_states, 1), stack(p_states, 2), stack(p_states, 3),
            stack(p_states, 4), stack(p_states, 5),
            stack(s_states, 0), stack(s_states, 1), stack(s_states, 2), stack(s_states, 3))
```

```python
import functools

import jax
import jax.numpy as jnp
import numpy as np
from jax import lax
from jax.experimental import pallas as pl
from jax.experimental.pallas import tpu as pltpu

F32 = jnp.float32
BF16 = jnp.bfloat16

D_MODEL = 2048
BATCH = 4
SEQ = 2048
DEPTH = 2
DEC_BATCH = 16
DEC_SEQ = 16
PAST_LEN = 4096
CHUNK = 64
N_MEM = 256
GLA_H = 4
GLA_DK = 128
GLA_DV = 256
GATE_RANK = 16
GATE_TAU = 16.0
DSA_H = 8
DSA_KVH = 2
DSA_HD = 128
IDX_H = 16
IDX_D = 64
TOPK_MAX = 256
XA_H = 4
XA_HD = D_MODEL // XA_H
N_EXPERTS = 16
N_GROUPS = 4
EXP_PER_GROUP = N_EXPERTS // N_GROUPS
D_FF = 512
ROPE_THETA = 10000.0
LN_EPS = 1e-5
DN_ALPHA = (2 * DEPTH) ** 0.25

IN_SPLITS = (GLA_H * GLA_DK, GLA_H * GLA_DK, GLA_H * GLA_DV, GATE_RANK, GLA_H * GLA_DV,
             DSA_H * DSA_HD, DSA_KVH * DSA_HD, DSA_KVH * DSA_HD, IDX_H * IDX_D, IDX_D, IDX_H)
SPLIT_POINTS = tuple(int(c) for c in np.cumsum(IN_SPLITS)[:-1])

N_PROMPT = BATCH * SEQ
N_SAMPLE = DEC_BATCH * DEC_SEQ
N_TOK = N_PROMPT + N_SAMPLE

LANE = 128
GLA_SUB = 16
VMEM_LIMIT = 48 * 1024 * 1024

G_Q, G_K, G_V, G_OG, G_DV, G_MISC = 0, 512, 1024, 2048, 3072, 3328
WG_COLS = 3456
WR_COLS = 1280
WI_COLS = 1152
MISC_LR = 0
MISC_IW = GATE_RANK


def _cparams(sem):
    return pltpu.CompilerParams(dimension_semantics=sem, vmem_limit_bytes=VMEM_LIMIT)


def _dot(a, b):
    return jnp.dot(a, b, preferred_element_type=F32)


def _dot_nt(a, b):
    return lax.dot_general(a, b, (((1,), (1,)), ((), ())), preferred_element_type=F32)


def _dot_tn(a, b):
    return lax.dot_general(a, b, (((0,), (0,)), ((), ())), preferred_element_type=F32)


def _layer_norm_rows(y, g, b):
    mu = jnp.mean(y, axis=-1, keepdims=True)
    d = y - mu
    var = jnp.mean(d * d, axis=-1, keepdims=True)
    return d * lax.rsqrt(var + LN_EPS) * g + b


def _seg_specs(segs, tm):
    specs, starts, start = [], [], 0
    for a in segs:
        nt = a.shape[0] // tm
        specs.append(pl.BlockSpec((tm, a.shape[1]),
                                  functools.partial(lambda i, s0, n: (jnp.clip(i - s0, 0, n - 1), 0), s0=start, n=nt)))
        starts.append(start)
        start += nt
    return specs, tuple(starts), start


def _seg_load(refs, starts):
    i = pl.program_id(0)
    v = refs[0][...]
    for r, s0 in zip(refs[1:], starts[1:]):
        v = jnp.where(i >= s0, r[...], v)
    return v


def _ln_kernel(*refs, starts):
    n = len(starts)
    g_ref, b_ref, o_ref, ob_ref = refs[n:]
    y = _layer_norm_rows(_seg_load(refs[:n], starts), g_ref[...], b_ref[...])
    o_ref[...] = y
    ob_ref[...] = y.astype(BF16)


def layer_norm_rows(segs, g, b, tm):
    d = segs[0].shape[1]
    specs, starts, nt = _seg_specs(segs, tm)
    return pl.pallas_call(
        functools.partial(_ln_kernel, starts=starts),
        grid=(nt,),
        in_specs=specs + [pl.BlockSpec((1, d), lambda i: (0, 0)),
                          pl.BlockSpec((1, d), lambda i: (0, 0))],
        out_specs=[pl.BlockSpec((tm, d), lambda i: (i, 0)),
                   pl.BlockSpec((tm, d), lambda i: (i, 0))],
        out_shape=[jax.ShapeDtypeStruct((nt * tm, d), F32), jax.ShapeDtypeStruct((nt * tm, d), BF16)],
        compiler_params=_cparams(("parallel",)),
        name="ln_in",
    )(*segs, g.reshape(1, d), b.reshape(1, d))


def _swap_halves(x, half):
    if half == LANE // 2:
        return pltpu.roll(x, half, 1)
    lane = lax.broadcasted_iota(jnp.int32, x.shape, 1)
    lower = (lane % (2 * half)) < half
    return jnp.where(lower, pltpu.roll(x, LANE - half, 1), pltpu.roll(x, half, 1))


def _mm_kernel(a_ref, w_ref, *rest, rope_half, tn):
    acc = _dot(a_ref[...], w_ref[...])
    if rope_half is None:
        (o_ref,) = rest
        o_ref[...] = acc.astype(o_ref.dtype)
        return
    cos_ref, sin_ref, o_ref = rest
    cos = cos_ref[...]
    sin = sin_ref[...]
    for c in range(tn // LANE):
        x = acc[:, c * LANE:(c + 1) * LANE]
        o_ref[:, c * LANE:(c + 1) * LANE] = (x * cos + _swap_halves(x, rope_half) * sin).astype(o_ref.dtype)


def matmul(a, w, *, tm, tn, out_dtype=F32, rope=None, name="mm"):
    m, k = a.shape
    n = w.shape[1]
    in_specs = [pl.BlockSpec((tm, k), lambda j, i: (i, 0)),
                pl.BlockSpec((k, tn), lambda j, i: (0, j))]
    args = [a, w]
    rope_half = None
    if rope is not None:
        cos, sin, rope_half = rope
        in_specs += [pl.BlockSpec((tm, LANE), lambda j, i: (i, 0)),
                     pl.BlockSpec((tm, LANE), lambda j, i: (i, 0))]
        args += [cos, sin]
    return pl.pallas_call(
        functools.partial(_mm_kernel, rope_half=rope_half, tn=tn),
        grid=(n // tn, m // tm),
        in_specs=in_specs,
        out_specs=pl.BlockSpec((tm, tn), lambda j, i: (i, j)),
        out_shape=jax.ShapeDtypeStruct((m, n), out_dtype),
        compiler_params=_cparams(("parallel", "parallel")),
        name=name,
    )(*args)


def _mm_ln_kernel(*refs, seg_starts):
    pos = 0
    a_vals = []
    for starts in seg_starts:
        a_vals.append(_seg_load(refs[pos:pos + len(starts)], starts))
        pos += len(starts)
    n_pairs = len(seg_starts)
    w_refs = refs[pos:pos + n_pairs]
    resid_ref, g_ref, b_ref, o_ref, ob_ref = refs[pos + n_pairs:]
    acc = _dot(a_vals[0], w_refs[0][...])
    for a, w_ref in zip(a_vals[1:], w_refs[1:]):
        acc = acc + _dot(a, w_ref[...])
    y = _layer_norm_rows(DN_ALPHA * resid_ref[...] + acc, g_ref[...], b_ref[...])
    o_ref[...] = y
    ob_ref[...] = y.astype(BF16)


def matmul_ln(a_list, w_list, resid, g, b, *, tm, name):
    m, d = resid.shape
    a_specs, seg_starts, a_args = [], [], []
    for segs in a_list:
        specs, starts, nt = _seg_specs(segs, tm)
        assert nt * tm == m
        a_specs += specs
        seg_starts.append(starts)
        a_args += list(segs)
    in_specs = (a_specs
                + [pl.BlockSpec(w.shape, lambda i: (0, 0)) for w in w_list]
                + [pl.BlockSpec((tm, d), lambda i: (i, 0)),
                   pl.BlockSpec((1, d), lambda i: (0, 0)),
                   pl.BlockSpec((1, d), lambda i: (0, 0))])
    return pl.pallas_call(
        functools.partial(_mm_ln_kernel, seg_starts=tuple(seg_starts)),
        grid=(m // tm,),
        in_specs=in_specs,
        out_specs=[pl.BlockSpec((tm, d), lambda i: (i, 0)),
                   pl.BlockSpec((tm, d), lambda i: (i, 0))],
        out_shape=[jax.ShapeDtypeStruct((m, d), F32), jax.ShapeDtypeStruct((m, d), BF16)],
        compiler_params=_cparams(("parallel",)),
        name=name,
    )(*a_args, *w_list, resid, g.reshape(1, d), b.reshape(1, d))


def _log_sigmoid(z):
    return jnp.minimum(z, 0.0) - jnp.log1p(jnp.exp(-jnp.abs(z)))


def _gla_kernel(*refs, tb, chunk, sub, has_state):
    if has_state:
        q_ref, k_ref, v_ref, og_ref, misc_ref, wa2_ref, ba_ref, gn_ref, s0_ref, o_ref, sout_ref, st_ref = refs
    else:
        q_ref, k_ref, v_ref, og_ref, misc_ref, wa2_ref, ba_ref, gn_ref, o_ref, sout_ref, st_ref = refs
    t = pl.program_id(2)

    @pl.when(t == 0)
    def _():
        if has_state:
            st_ref[...] = s0_ref[...].T
        else:
            st_ref[...] = jnp.zeros_like(st_ref)

    z = jnp.dot(misc_ref[...], wa2_ref[...], preferred_element_type=F32,
                precision=lax.Precision.HIGHEST) + ba_ref[...]
    g_all = _log_sigmoid(z) * (1.0 / GATE_TAU)

    row = lax.broadcasted_iota(jnp.int32, (chunk, chunk), 0)
    col = lax.broadcasted_iota(jnp.int32, (chunk, chunk), 1)
    tril = (row >= col).astype(F32)
    gn = gn_ref[...]

    for c in range(tb // chunk):
        rows = slice(c * chunk, (c + 1) * chunk)
        G = jnp.dot(tril, g_all[rows], preferred_element_type=F32,
                    precision=lax.Precision.HIGHEST)
        qc = q_ref[rows, :] * (GLA_DK ** -0.5)
        kc = k_ref[rows, :]
        vb = v_ref[rows, :].astype(BF16)
        st = st_ref[...]
        o = _dot_nt((qc * jnp.exp(G)).astype(BF16), st.astype(BF16))
        o_parts = []
        for i in range(chunk // sub):
            r0 = i * sub
            nk = r0 + sub
            b_i = G[r0:r0 + 1, :]
            qt = qc[r0:nk] * jnp.exp(G[r0:nk] - b_i)
            kt = kc[:nk] * jnp.exp(b_i - G[:nk])
            a = _dot_nt(qt.astype(BF16), kt.astype(BF16))
            t_idx = r0 + lax.broadcasted_iota(jnp.int32, (sub, nk), 0)
            s_idx = lax.broadcasted_iota(jnp.int32, (sub, nk), 1)
            a = jnp.where(t_idx >= s_idx, a, 0.0)
            o_parts.append(_dot(a.astype(BF16), vb[:nk]))
        o = o + jnp.concatenate(o_parts, axis=0)
        g_last = G[chunk - 1:chunk, :]
        k_dec = kc * jnp.exp(g_last - G)
        st_ref[...] = st * jnp.exp(g_last) + _dot_tn(vb, k_dec.astype(BF16))
        n = o * lax.rsqrt(jnp.mean(o * o, axis=-1, keepdims=True) + LN_EPS) * gn
        og = og_ref[rows, :]
        o_ref[rows, :] = (n * (og * jax.nn.sigmoid(og))).astype(o_ref.dtype)

    @pl.when(t == pl.num_programs(2) - 1)
    def _():
        sout_ref[...] = st_ref[...].T


def gla_mix(pg, wa2p, ba, gn, s0, *, nb, seq, row0, tb, chunk, sub):
    nt = seq // tb
    rb0 = row0 // tb

    def tok(colblock):
        return lambda b, h, t: (rb0 + b * nt + t, colblock(h))

    in_specs = [
        pl.BlockSpec((tb, GLA_DK), tok(lambda h: G_Q // GLA_DK + h)),
        pl.BlockSpec((tb, GLA_DK), tok(lambda h: G_K // GLA_DK + h)),
        pl.BlockSpec((tb, GLA_DV), tok(lambda h: G_V // GLA_DV + h)),
        pl.BlockSpec((tb, GLA_DV), tok(lambda h: G_OG // GLA_DV + h)),
        pl.BlockSpec((tb, LANE), tok(lambda h: G_MISC // LANE)),
        pl.BlockSpec((None, LANE, GLA_DK), lambda b, h, t: (h, 0, 0)),
        pl.BlockSpec((None, 1, GLA_DK), lambda b, h, t: (h, 0, 0)),
        pl.BlockSpec((1, GLA_DV), lambda b, h, t: (0, 0)),
    ]
    args = [pg, pg, pg, pg, pg, wa2p, ba, gn]
    if s0 is not None:
        in_specs.append(pl.BlockSpec((None, None, GLA_DK, GLA_DV), lambda b, h, t: (b, h, 0, 0)))
        args.append(s0)
    return pl.pallas_call(
        functools.partial(_gla_kernel, tb=tb, chunk=chunk, sub=sub, has_state=s0 is not None),
        grid=(nb, GLA_H, nt),
        in_specs=in_specs,
        out_specs=[pl.BlockSpec((tb, GLA_DV), lambda b, h, t: (b * nt + t, h)),
                   pl.BlockSpec((None, None, GLA_DK, GLA_DV), lambda b, h, t: (b, h, 0, 0))],
        out_shape=[jax.ShapeDtypeStruct((nb * seq, GLA_H * GLA_DV), BF16),
                   jax.ShapeDtypeStruct((nb, GLA_H, GLA_DK, GLA_DV), F32)],
        scratch_shapes=[pltpu.VMEM((GLA_DV, GLA_DK), F32)],
        compiler_params=_cparams(("parallel", "parallel", "arbitrary")),
        name="gla",
    )(*args)


INT_MIN = -2 ** 31
NEG_INF = float("-inf")


def _order_key(score):
    bits = pltpu.bitcast(score, jnp.int32)
    return jnp.where(bits < 0, bits ^ jnp.int32(0x7FFFFFFF), bits)


def _count(mask):
    return jnp.sum(jnp.where(mask, 1.0, 0.0), axis=1, keepdims=True)


def _topk_bias(score, adm, kidx, key_ref, topk):
    nq, n = score.shape
    idx_bits = int(np.ceil(np.log2(n + 1)))
    score = jnp.where(adm, score, NEG_INF)
    key_ref[...] = _order_key(score)

    def thr_body(i, t):
        cand = t + lax.shift_left(jnp.int32(1), 31 - i)
        return jnp.where(_count(key_ref[...] >= cand) >= float(topk), cand, t)

    thr = lax.fori_loop(0, 32, thr_body, jnp.full((nq, 1), INT_MIN, jnp.int32))

    key = key_ref[...]
    above = key > thr
    tied = key == thr
    need = float(topk) - _count(above)
    key_neg_inf = jnp.int32(np.array(-np.inf, np.float32).view(np.int32) ^ 0x7FFFFFFF)
    tie_rows = (_count(tied) > need) & (thr > key_neg_inf)
    any_tie = jnp.max(jnp.where(tie_rows, 1.0, 0.0)) > 0.0

    def tie_cut():
        def cut_body(i, j):
            cand = j + lax.shift_left(jnp.int32(1), idx_bits - 1 - i)
            return jnp.where(_count((key_ref[...] == thr) & (kidx < cand)) <= need, cand, j)
        return lax.fori_loop(0, idx_bits, cut_body, jnp.zeros((nq, 1), jnp.int32))

    cut = lax.cond(any_tie, tie_cut, lambda: jnp.full((nq, 1), 2 ** idx_bits, jnp.int32))
    sel = (above | (tied & (kidx < cut))) & adm & (score < float("inf"))
    return jnp.where(sel, 0.0, NEG_INF)


def _idx_weight(misc, h):
    return misc[:, MISC_IW + h:MISC_IW + h + 1] * (IDX_H ** -0.5 * IDX_D ** -0.5)


def _softmax_pv(s, vb):
    m = jnp.max(s, axis=1, keepdims=True)
    p = jnp.exp(s - m)
    l = jnp.sum(p, axis=1, keepdims=True)
    return _dot(p.astype(BF16), vb) / l


def _dsa_prompt_kernel(q_ref, qi_ref, misc_ref, k_ref, v_ref, ki_ref, o_ref, key_ref,
                       *, nq, seq, chunk, topk, step):
    qb = pl.program_id(1)
    lane = lax.broadcasted_iota(jnp.int32, (nq, LANE), 1)
    upper = lane >= IDX_D
    misc = misc_ref[...]
    groups = DSA_H // DSA_KVH

    def body(n):
        kib = ki_ref[pl.ds(0, n), :].astype(BF16)
        score = jnp.zeros((nq, n), F32)
        for h in range(IDX_H):
            pair = qi_ref[:, (h // 2) * LANE:(h // 2 + 1) * LANE]
            qm = jnp.where(upper == (h % 2 == 1), pair, 0.0).astype(BF16)
            score = score + jnp.maximum(_dot_nt(qm, kib), 0.0) * _idx_weight(misc, h)
        kidx = lax.broadcasted_iota(jnp.int32, (nq, n), 1)
        qpos = qb * nq + lax.broadcasted_iota(jnp.int32, (nq, 1), 0)
        limit = (lax.shift_right_logical(qpos, int(np.log2(chunk))) + 1) * chunk
        bias = _topk_bias(score, kidx < limit, kidx, key_ref.at[:, pl.ds(0, n)], topk)
        for j in range(DSA_KVH):
            kb = k_ref[pl.ds(0, n), j * DSA_HD:(j + 1) * DSA_HD].astype(BF16)
            vb = v_ref[pl.ds(0, n), j * DSA_HD:(j + 1) * DSA_HD].astype(BF16)
            for g in range(groups):
                h = j * groups + g
                qh = q_ref[:, h * DSA_HD:(h + 1) * DSA_HD].astype(BF16)
                s = _dot_nt(qh, kb) * (DSA_HD ** -0.5) + bias
                o_ref[:, h * DSA_HD:(h + 1) * DSA_HD] = _softmax_pv(s, vb).astype(o_ref.dtype)

    n_need = (qb + 1) * nq
    cls = (n_need + step - 1) // step - 1
    for c in range(seq // step):
        pl.when(cls == c)(functools.partial(body, (c + 1) * step))


def dsa_prompt(pr, pi, pg, *, nb, seq, nq, chunk, topk, step):
    nqb = seq // nq
    kw = DSA_KVH * DSA_HD
    qmap = lambda cb: (lambda b, t: (b * nqb + t, cb))
    return pl.pallas_call(
        functools.partial(_dsa_prompt_kernel, nq=nq, seq=seq, chunk=chunk, topk=topk, step=step),
        grid=(nb, nqb),
        in_specs=[pl.BlockSpec((nq, DSA_H * DSA_HD), qmap(0)),
                  pl.BlockSpec((nq, IDX_H * IDX_D), qmap(0)),
                  pl.BlockSpec((nq, LANE), qmap(G_MISC // LANE)),
                  pl.BlockSpec((seq, kw), lambda b, t: (b, DSA_H * DSA_HD // kw)),
                  pl.BlockSpec((seq, kw), lambda b, t: (b, G_DV // kw)),
                  pl.BlockSpec((seq, LANE), lambda b, t: (b, IDX_H * IDX_D // LANE))],
        out_specs=pl.BlockSpec((nq, DSA_H * DSA_HD), lambda b, t: (b * nqb + t, 0)),
        out_shape=jax.ShapeDtypeStruct((nb * seq, DSA_H * DSA_HD), BF16),
        scratch_shapes=[pltpu.VMEM((nq, seq), jnp.int32)],
        compiler_params=_cparams(("parallel", "arbitrary")),
        name="dsa_prompt",
    )(pr, pi, pg, pr, pg, pi)


def _dsa_sample_kernel(q_ref, qi_ref, misc_ref, kn_ref, vn_ref, kin_ref, ck_ref, cv_ref, cik_ref, o_ref, key_ref,
                       *, nq, past, topk):
    misc = misc_ref[...]
    groups = DSA_H // DSA_KVH
    zrows = LANE - nq
    cikb = cik_ref[...].astype(BF16)
    kinb = jnp.concatenate([kin_ref[:, :IDX_D], jnp.zeros((zrows, IDX_D), F32)], 0).astype(BF16)
    qs = jnp.concatenate([qi_ref[:, h * IDX_D:(h + 1) * IDX_D] for h in range(IDX_H)], 0).astype(BF16)
    lg_c = _dot_nt(qs, cikb)
    lg_n = _dot_nt(qs, kinb)
    sc_c = jnp.zeros((nq, past), F32)
    sc_n = jnp.zeros((nq, LANE), F32)
    for h in range(IDX_H):
        w = _idx_weight(misc, h)
        sc_c = sc_c + jnp.maximum(lg_c[h * nq:(h + 1) * nq], 0.0) * w
        sc_n = sc_n + jnp.maximum(lg_n[h * nq:(h + 1) * nq], 0.0) * w
    n = past + LANE
    score = jnp.concatenate([sc_c, sc_n], 1)
    kidx = lax.broadcasted_iota(jnp.int32, (nq, n), 1)
    bias = _topk_bias(score, kidx < past + nq, kidx, key_ref, topk)
    bias_g = jnp.concatenate([bias] * groups, 0)
    for j in range(DSA_KVH):
        cols = slice(j * DSA_HD, (j + 1) * DSA_HD)
        kb_c = ck_ref[pl.ds(j, past, stride=DSA_KVH), :].astype(BF16)
        vb_c = cv_ref[pl.ds(j, past, stride=DSA_KVH), :].astype(BF16)
        kb_n = jnp.concatenate([kn_ref[:, cols], jnp.zeros((zrows, DSA_HD), F32)], 0).astype(BF16)
        vb_n = jnp.concatenate([vn_ref[:, cols], jnp.zeros((zrows, DSA_HD), F32)], 0).astype(BF16)
        qg = jnp.concatenate([q_ref[:, (j * groups + g) * DSA_HD:(j * groups + g + 1) * DSA_HD]
                              for g in range(groups)], 0).astype(BF16)
        s = jnp.concatenate([_dot_nt(qg, kb_c), _dot_nt(qg, kb_n)], 1) * (DSA_HD ** -0.5) + bias_g
        m = jnp.max(s, axis=1, keepdims=True)
        p = jnp.exp(s - m)
        l = jnp.sum(p, axis=1, keepdims=True)
        pb = p.astype(BF16)
        o = (_dot(pb[:, :past], vb_c) + _dot(pb[:, past:], vb_n)) / l
        for g in range(groups):
            h = j * groups + g
            o_ref[:, h * DSA_HD:(h + 1) * DSA_HD] = o[g * nq:(g + 1) * nq].astype(o_ref.dtype)


def dsa_sample(pr, pi, pg, cache_k, cache_v, cache_ik, layer, *, nb, nq, row0, past, topk):
    rb0 = row0 // nq
    kw = DSA_KVH * DSA_HD
    smap = lambda cb: (lambda b: (rb0 + b, cb))
    depth = cache_k.shape[0]
    cache_k = cache_k.reshape(depth, nb, past * DSA_KVH, DSA_HD)
    cache_v = cache_v.reshape(depth, nb, past * DSA_KVH, DSA_HD)
    kv_spec = pl.BlockSpec((None, None, past * DSA_KVH, DSA_HD), lambda b: (layer, b, 0, 0))
    return pl.pallas_call(
        functools.partial(_dsa_sample_kernel, nq=nq, past=past, topk=topk),
        grid=(nb,),
        in_specs=[pl.BlockSpec((nq, DSA_H * DSA_HD), smap(0)),
                  pl.BlockSpec((nq, IDX_H * IDX_D), smap(0)),
                  pl.BlockSpec((nq, LANE), smap(G_MISC // LANE)),
                  pl.BlockSpec((nq, kw), smap(DSA_H * DSA_HD // kw)),
                  pl.BlockSpec((nq, kw), smap(G_DV // kw)),
                  pl.BlockSpec((nq, LANE), smap(IDX_H * IDX_D // LANE))]
                 + [kv_spec, kv_spec, pl.BlockSpec((None, None, past, IDX_D), lambda b: (layer, b, 0, 0))],
        out_specs=pl.BlockSpec((nq, DSA_H * DSA_HD), lambda b: (b, 0)),
        out_shape=jax.ShapeDtypeStruct((nb * nq, DSA_H * DSA_HD), BF16),
        scratch_shapes=[pltpu.VMEM((nq, past + LANE), jnp.int32)],
        compiler_params=_cparams(("parallel",)),
        name="dsa_sample",
    )(pr, pi, pg, pr, pg, pi, cache_k, cache_v, cache_ik)


def _xattn_kernel(q_ref, mk_ref, mv_ref, o_ref, *, heads_split):
    for h in range(XA_H):
        cols = slice(h * XA_HD, (h + 1) * XA_HD)
        mk = mk_ref[:, h, :] if heads_split else mk_ref[:, cols]
        mv = mv_ref[:, h, :] if heads_split else mv_ref[:, cols]
        s = _dot_nt(q_ref[:, cols], mk.astype(BF16)) * (XA_HD ** -0.5)
        o_ref[:, cols] = _softmax_pv(s, mv.astype(BF16)).astype(o_ref.dtype)


def cross_attend(q, mk, mv, mem_index, mem_block, *, nb, seq, row0, tq):
    nt = seq // tq
    rb0 = row0 // tq
    d = q.shape[1]
    mem_spec = pl.BlockSpec(mem_block, lambda b, t: mem_index(b))
    return pl.pallas_call(
        functools.partial(_xattn_kernel, heads_split=len([s for s in mem_block if s is not None]) == 3),
        grid=(nb, nt),
        in_specs=[pl.BlockSpec((tq, d), lambda b, t: (rb0 + b * nt + t, 0)), mem_spec, mem_spec],
        out_specs=pl.BlockSpec((tq, d), lambda b, t: (b * nt + t, 0)),
        out_shape=jax.ShapeDtypeStruct((nb * seq, d), BF16),
        compiler_params=_cparams(("parallel", "parallel")),
        name="xattn",
    )(q, mk, mv)


def _top2_sum(vals):
    a, b, c, d = vals
    m1, n1 = jnp.maximum(a, b), jnp.minimum(a, b)
    m2, n2 = jnp.maximum(c, d), jnp.minimum(c, d)
    return jnp.maximum(m1, m2) + jnp.maximum(jnp.minimum(m1, m2), jnp.maximum(n1, n2))


def _router_kernel(x_ref, rwt_ref, rb_ref, comb_ref):
    logits = lax.dot_general(rwt_ref[...], x_ref[...], (((1,), (1,)), ((), ())),
                             preferred_element_type=F32, precision=lax.Precision.HIGHEST)
    aff = jax.nn.sigmoid(logits)
    biased = aff + rb_ref[...]
    a = [aff[e:e + 1, :] for e in range(N_EXPERTS)]
    b = [biased[e:e + 1, :] for e in range(N_EXPERTS)]
    gs = [_top2_sum(b[g * EXP_PER_GROUP:(g + 1) * EXP_PER_GROUP]) for g in range(N_GROUPS)]
    one = jnp.ones_like(a[0])
    zero = jnp.zeros_like(a[0])
    gated = []
    for g in range(N_GROUPS):
        win = one
        for g2 in range(N_GROUPS):
            if g2 < g:
                win = win * jnp.where(gs[g] > gs[g2], one, zero)
            elif g2 > g:
                win = win * jnp.where(gs[g] >= gs[g2], one, zero)
        for e in range(g * EXP_PER_GROUP, (g + 1) * EXP_PER_GROUP):
            beaten = zero
            for f in range(g * EXP_PER_GROUP, (g + 1) * EXP_PER_GROUP):
                if f < e:
                    beaten = beaten + jnp.where(b[f] >= b[e], one, zero)
                elif f > e:
                    beaten = beaten + jnp.where(b[f] > b[e], one, zero)
            gated.append(jnp.where(beaten < 2.0, win, zero) * a[e])
    denom = gated[0]
    for t in gated[1:]:
        denom = denom + t
    comb_ref[...] = jnp.concatenate(gated, axis=0) / denom


def moe_router(x, rwt, rb, *, tm):
    n, d = x.shape
    return pl.pallas_call(
        _router_kernel,
        grid=(n // tm,),
        in_specs=[pl.BlockSpec((tm, d), lambda i: (i, 0)),
                  pl.BlockSpec((N_EXPERTS, d), lambda i: (0, 0)),
                  pl.BlockSpec((N_EXPERTS, 1), lambda i: (0, 0))],
        out_specs=pl.BlockSpec((N_EXPERTS, tm), lambda i: (0, i)),
        out_shape=jax.ShapeDtypeStruct((N_EXPERTS, n), F32),
        compiler_params=_cparams(("parallel",)),
        name="router",
    )(x, rwt, rb)


def _moe_kernel(xb_ref, x_ref, comb_ref, wg_ref, wu_ref, wd_ref, g_ref, b_ref, o_ref, ob_ref, acc_ref):
    e = pl.program_id(1)

    @pl.when(e == 0)
    def _():
        acc_ref[...] = jnp.zeros_like(acc_ref)

    xb = xb_ref[...]
    hg = _dot(xb, wg_ref[...])
    hu = _dot(xb, wu_ref[...])
    comb = comb_ref[...]
    lane = lax.broadcasted_iota(jnp.int32, comb.shape, 1)
    c = jnp.sum(jnp.where(lane == e, comb, 0.0), axis=1, keepdims=True)
    hid = (hg * jax.nn.sigmoid(hg)) * hu * c
    acc_ref[...] += _dot(hid.astype(BF16), wd_ref[...])

    @pl.when(e == pl.num_programs(1) - 1)
    def _():
        y = _layer_norm_rows(DN_ALPHA * x_ref[...] + acc_ref[...], g_ref[...], b_ref[...])
        o_ref[...] = y
        ob_ref[...] = y.astype(BF16)


def moe_ffn_ln(xb, x, comb, wg, wu, wd, g, b, *, tm):
    n, d = x.shape
    return pl.pallas_call(
        _moe_kernel,
        grid=(n // tm, N_EXPERTS),
        in_specs=[pl.BlockSpec((tm, d), lambda i, e: (i, 0)),
                  pl.BlockSpec((tm, d), lambda i, e: (i, 0)),
                  pl.BlockSpec((tm, N_EXPERTS), lambda i, e: (i, 0)),
                  pl.BlockSpec((None, d, D_FF), lambda i, e: (e, 0, 0)),
                  pl.BlockSpec((None, d, D_FF), lambda i, e: (e, 0, 0)),
                  pl.BlockSpec((None, D_FF, d), lambda i, e: (e, 0, 0)),
                  pl.BlockSpec((1, d), lambda i, e: (0, 0)),
                  pl.BlockSpec((1, d), lambda i, e: (0, 0))],
        out_specs=[pl.BlockSpec((tm, d), lambda i, e: (i, 0)),
                   pl.BlockSpec((tm, d), lambda i, e: (i, 0))],
        out_shape=[jax.ShapeDtypeStruct((n, d), F32), jax.ShapeDtypeStruct((n, d), BF16)],
        scratch_shapes=[pltpu.VMEM((tm, d), F32)],
        compiler_params=_cparams(("parallel", "arbitrary")),
        name="moe",
    )(xb, x, comb, wg, wu, wd, g.reshape(1, d), b.reshape(1, d))


def _rope_tables(pos, head_dim):
    half = head_dim // 2
    inv_freq = ROPE_THETA ** (-jnp.arange(half, dtype=F32) / half)
    ang = pos.astype(F32)[:, None] * inv_freq[None, :]
    cos, sin = jnp.cos(ang), jnp.sin(ang)
    reps = LANE // head_dim
    return (jnp.tile(jnp.concatenate([cos, cos], -1), (1, reps)),
            jnp.tile(jnp.concatenate([-sin, sin], -1), (1, reps)))


def _pack_w_in(w):
    gq, gk, gv, glr, gog, dq, dk, dv, iq, ik, iw = jnp.split(w, SPLIT_POINTS, axis=-1)
    pad = jnp.zeros((w.shape[0], LANE - GATE_RANK - IDX_H), w.dtype)
    wg = jnp.concatenate([gq, gk, gv, gog, dv, glr, iw, pad], -1).astype(BF16)
    wr = jnp.concatenate([dq, dk], -1).astype(BF16)
    wi = jnp.concatenate([iq, ik, ik], -1).astype(BF16)
    return wg, wr, wi


def _pack_wa2(wa2):
    w = wa2.reshape(GATE_RANK, GLA_H, GLA_DK).transpose(1, 0, 2)
    return jnp.pad(w, ((0, 0), (MISC_LR, LANE - GATE_RANK - MISC_LR), (0, 0)))


def kernel(x_prompt, x_sample, cache_dsa_k, cache_dsa_v, cache_idx_k, state_gla, cache_mem_k, cache_mem_v,
           mem_prompt, ln_in_g, ln_in_b, w_in, gla_wa2, gla_ba, gla_norm_g, w_out, ln1_g, ln1_b,
           xa_wq, xa_wk, xa_wv, xa_wo, ln2_g, ln2_b, router_w, router_b,
           moe_w_gate, moe_w_up, moe_w_down, ln3_g, ln3_b):
    tm = 768
    tm_ln = 256
    x, xb = layer_norm_rows([x_prompt.reshape(N_PROMPT, D_MODEL), x_sample.reshape(N_SAMPLE, D_MODEL)],
                            ln_in_g, ln_in_b, tm_ln)

    pos = jnp.concatenate([jnp.tile(jnp.arange(SEQ), BATCH),
                           jnp.tile(PAST_LEN + jnp.arange(DEC_SEQ), DEC_BATCH)])
    rope128 = _rope_tables(pos, DSA_HD) + (DSA_HD // 2,)
    rope64 = _rope_tables(pos, IDX_D) + (IDX_D // 2,)
    memb = mem_prompt.reshape(BATCH * N_MEM, D_MODEL).astype(BF16)
    rwt = router_w.T
    rb = router_b.reshape(N_EXPERTS, 1)

    p_states, s_states = [], []
    for l in range(DEPTH):
        wg, wr, wi = _pack_w_in(w_in[l])
        pg = matmul(xb, wg, tm=tm, tn=1152, name="in_proj_g")
        pr = matmul(xb, wr, tm=tm, tn=WR_COLS, rope=rope128, name="in_proj_r")
        pi = matmul(xb, wi, tm=tm, tn=WI_COLS, rope=rope64, name="in_proj_i")

        wa2p = _pack_wa2(gla_wa2[l])
        ba = gla_ba[l].reshape(GLA_H, 1, GLA_DK)
        gn = gla_norm_g[l].reshape(1, GLA_DV)
        mixg_p, st_p = gla_mix(pg, wa2p, ba, gn, None, nb=BATCH, seq=SEQ, row0=0,
                               tb=256, chunk=CHUNK, sub=GLA_SUB)
        mixg_s, st_s = gla_mix(pg, wa2p, ba, gn, state_gla[l], nb=DEC_BATCH, seq=DEC_SEQ, row0=N_PROMPT,
                               tb=DEC_SEQ, chunk=DEC_SEQ, sub=DEC_SEQ)

        od_p = dsa_prompt(pr, pi, pg, nb=BATCH, seq=SEQ, nq=128, chunk=CHUNK,
                          topk=min(TOPK_MAX, SEQ // 4), step=1024)
        od_s = dsa_sample(pr, pi, pg, cache_dsa_k, cache_dsa_v, cache_idx_k, l, nb=DEC_BATCH, nq=DEC_SEQ,
                          row0=N_PROMPT, past=PAST_LEN, topk=min(TOPK_MAX, (PAST_LEN + DEC_SEQ) // 4))

        wo = w_out[l].astype(BF16)
        x1, x1b = matmul_ln([[mixg_p, mixg_s], [od_p, od_s]], [wo[:GLA_H * GLA_DV], wo[GLA_H * GLA_DV:]],
                            x, ln1_g[l], ln1_b[l], tm=tm_ln, name="out_proj_ln1")

        q_xa = matmul(x1b, xa_wq[l].astype(BF16), tm=tm, tn=1024, out_dtype=BF16, name="xa_q")
        mk_p = matmul(memb, xa_wk[l].astype(BF16), tm=512, tn=1024, name="mem_k")
        mv_p = matmul(memb, xa_wv[l].astype(BF16), tm=512, tn=1024, name="mem_v")
        xa_p = cross_attend(q_xa, mk_p, mv_p, lambda b: (b, 0), (N_MEM, D_MODEL),
                            nb=BATCH, seq=SEQ, row0=0, tq=512)
        xa_s = cross_attend(q_xa, cache_mem_k, cache_mem_v, lambda b, l=l: (l, b, 0, 0, 0),
                            (None, None, N_MEM, XA_H, XA_HD), nb=DEC_BATCH, seq=DEC_SEQ, row0=N_PROMPT, tq=DEC_SEQ)
        x2, x2b = matmul_ln([[xa_p, xa_s]], [xa_wo[l].astype(BF16)], x1, ln2_g[l], ln2_b[l],
                            tm=tm_ln, name="xa_o_ln2")

        comb = moe_router(x2, rwt, rb, tm=tm).T
        x, xb = moe_ffn_ln(x2b, x2, comb, moe_w_gate[l].astype(BF16), moe_w_up[l].astype(BF16),
                           moe_w_down[l].astype(BF16), ln3_g[l], ln3_b[l], tm=384)

        kv = DSA_KVH * DSA_HD
        p_states.append((pr[:N_PROMPT, DSA_H * DSA_HD:].reshape(BATCH, SEQ, DSA_KVH, DSA_HD),
                         pg[:N_PROMPT, G_DV:G_DV + kv].reshape(BATCH, SEQ, DSA_KVH, DSA_HD),
                         pi[:N_PROMPT, IDX_H * IDX_D:IDX_H * IDX_D + IDX_D].reshape(BATCH, SEQ, IDX_D),
                         st_p,
                         mk_p.reshape(BATCH, N_MEM, XA_H, XA_HD),
                         mv_p.reshape(BATCH, N_MEM, XA_H, XA_HD)))
        s_states.append((pr[N_PROMPT:, DSA_H * DSA_HD:].reshape(DEC_BATCH, DEC_SEQ, DSA_KVH, DSA_HD),
                         pg[N_PROMPT:, G_DV:G_DV + kv].reshape(DEC_BATCH, DEC_SEQ, DSA_KVH, DSA_HD),
                         pi[N_PROMPT:, IDX_H * IDX_D:IDX_H * IDX_D + IDX_D].reshape(DEC_BATCH, DEC_SEQ, IDX_D),
                         st_s))

    def stack(states, i):
        return jnp.stack([st[i] for st in states])

    return (x[:N_PROMPT].reshape(BATCH, SEQ, D_MODEL), x[N_PROMPT:].reshape(DEC_BATCH, DEC_SEQ, D_MODEL),
            stack(p_states, 0), stack(p_states, 1), stack(p_states, 2), stack(p_states, 3),
            stack(p_states, 4), stack(p_states, 5),
            stack(s_states, 0), stack(s_states, 1), stack(s_states, 2), stack(s_states, 3))
```

```python
import functools

import jax
import jax.numpy as jnp
import numpy as np
from jax import lax
from jax.experimental import pallas as pl
from jax.experimental.pallas import tpu as pltpu

F32 = jnp.float32
BF16 = jnp.bfloat16

D_MODEL = 2048
BATCH = 4
SEQ = 2048
DEPTH = 2
DEC_BATCH = 16
DEC_SEQ = 16
PAST_LEN = 4096
CHUNK = 64
N_MEM = 256
GLA_H = 4
GLA_DK = 128
GLA_DV = 256
GATE_RANK = 16
GATE_TAU = 16.0
DSA_H = 8
DSA_KVH = 2
DSA_HD = 128
IDX_H = 16
IDX_D = 64
TOPK_MAX = 256
XA_H = 4
XA_HD = D_MODEL // XA_H
N_EXPERTS = 16
N_GROUPS = 4
EXP_PER_GROUP = N_EXPERTS // N_GROUPS
D_FF = 512
ROPE_THETA = 10000.0
LN_EPS = 1e-5
DN_ALPHA = (2 * DEPTH) ** 0.25

IN_SPLITS = (GLA_H * GLA_DK, GLA_H * GLA_DK, GLA_H * GLA_DV, GATE_RANK, GLA_H * GLA_DV,
             DSA_H * DSA_HD, DSA_KVH * DSA_HD, DSA_KVH * DSA_HD, IDX_H * IDX_D, IDX_D, IDX_H)
SPLIT_POINTS = tuple(int(c) for c in np.cumsum(IN_SPLITS)[:-1])

N_PROMPT = BATCH * SEQ
N_SAMPLE = DEC_BATCH * DEC_SEQ
N_TOK = N_PROMPT + N_SAMPLE

LANE = 128
GLA_SUB = 16
VMEM_LIMIT = 48 * 1024 * 1024

G_Q, G_K, G_V, G_OG, G_DV, G_MISC = 0, 512, 1024, 2048, 3072, 3328
WG_COLS = 3456
WR_COLS = 1280
WI_COLS = 1152
MISC_LR = 0
MISC_IW = GATE_RANK


def _cparams(sem):
    return pltpu.CompilerParams(dimension_semantics=sem, vmem_limit_bytes=VMEM_LIMIT)


def _dot(a, b):
    return jnp.dot(a, b, preferred_element_type=F32)


def _dot_nt(a, b):
    return lax.dot_general(a, b, (((1,), (1,)), ((), ())), preferred_element_type=F32)


def _dot_tn(a, b):
    return lax.dot_general(a, b, (((0,), (0,)), ((), ())), preferred_element_type=F32)


def _layer_norm_rows(y, g, b):
    mu = jnp.mean(y, axis=-1, keepdims=True)
    d = y - mu
    var = jnp.mean(d * d, axis=-1, keepdims=True)
    return d * lax.rsqrt(var + LN_EPS) * g + b


def _seg_specs(segs, tm):
    specs, starts, start = [], [], 0
    for a in segs:
        nt = a.shape[0] // tm
        specs.append(pl.BlockSpec((tm, a.shape[1]),
                                  functools.partial(lambda i, s0, n: (jnp.clip(i - s0, 0, n - 1), 0), s0=start, n=nt)))
        starts.append(start)
        start += nt
    return specs, tuple(starts), start


def _seg_load(refs, starts):
    i = pl.program_id(0)
    v = refs[0][...]
    for r, s0 in zip(refs[1:], starts[1:]):
        v = jnp.where(i >= s0, r[...], v)
    return v


def _ln_kernel(*refs, starts):
    n = len(starts)
    g_ref, b_ref, o_ref, ob_ref = refs[n:]
    y = _layer_norm_rows(_seg_load(refs[:n], starts), g_ref[...], b_ref[...])
    o_ref[...] = y
    ob_ref[...] = y.astype(BF16)


def layer_norm_rows(segs, g, b, tm):
    d = segs[0].shape[1]
    specs, starts, nt = _seg_specs(segs, tm)
    return pl.pallas_call(
        functools.partial(_ln_kernel, starts=starts),
        grid=(nt,),
        in_specs=specs + [pl.BlockSpec((1, d), lambda i: (0, 0)),
                          pl.BlockSpec((1, d), lambda i: (0, 0))],
        out_specs=[pl.BlockSpec((tm, d), lambda i: (i, 0)),
                   pl.BlockSpec((tm, d), lambda i: (i, 0))],
        out_shape=[jax.ShapeDtypeStruct((nt * tm, d), F32), jax.ShapeDtypeStruct((nt * tm, d), BF16)],
        compiler_params=_cparams(("parallel",)),
        name="ln_in",
    )(*segs, g.reshape(1, d), b.reshape(1, d))


def _swap_halves(x, half):
    if half == LANE // 2:
        return pltpu.roll(x, half, 1)
    lane = lax.broadcasted_iota(jnp.int32, x.shape, 1)
    lower = (lane % (2 * half)) < half
    return jnp.where(lower, pltpu.roll(x, LANE - half, 1), pltpu.roll(x, half, 1))


def _mm_kernel(a_ref, w_ref, *rest, rope_half, tn):
    acc = _dot(a_ref[...], w_ref[...])
    if rope_half is None:
        (o_ref,) = rest
        o_ref[...] = acc.astype(o_ref.dtype)
        return
    cos_ref, sin_ref, o_ref = rest
    cos = cos_ref[...]
    sin = sin_ref[...]
    for c in range(tn // LANE):
        x = acc[:, c * LANE:(c + 1) * LANE]
        o_ref[:, c * LANE:(c + 1) * LANE] = (x * cos + _swap_halves(x, rope_half) * sin).astype(o_ref.dtype)


def matmul(a, w, *, tm, tn, out_dtype=F32, rope=None, name="mm"):
    m, k = a.shape
    n = w.shape[1]
    in_specs = [pl.BlockSpec((tm, k), lambda j, i: (i, 0)),
                pl.BlockSpec((k, tn), lambda j, i: (0, j))]
    args = [a, w]
    rope_half = None
    if rope is not None:
        cos, sin, rope_half = rope
        in_specs += [pl.BlockSpec((tm, LANE), lambda j, i: (i, 0)),
                     pl.BlockSpec((tm, LANE), lambda j, i: (i, 0))]
        args += [cos, sin]
    return pl.pallas_call(
        functools.partial(_mm_kernel, rope_half=rope_half, tn=tn),
        grid=(n // tn, m // tm),
        in_specs=in_specs,
        out_specs=pl.BlockSpec((tm, tn), lambda j, i: (i, j)),
        out_shape=jax.ShapeDtypeStruct((m, n), out_dtype),
        compiler_params=_cparams(("parallel", "parallel")),
        name=name,
    )(*args)


def _mm_ln_kernel(*refs, seg_starts):
    pos = 0
    a_vals = []
    for starts in seg_starts:
        a_vals.append(_seg_load(refs[pos:pos + len(starts)], starts))
        pos += len(starts)
    n_pairs = len(seg_starts)
    w_refs = refs[pos:pos + n_pairs]
    resid_ref, g_ref, b_ref, o_ref, ob_ref = refs[pos + n_pairs:]
    acc = _dot(a_vals[0], w_refs[0][...])
    for a, w_ref in zip(a_vals[1:], w_refs[1:]):
        acc = acc + _dot(a, w_ref[...])
    y = _layer_norm_rows(DN_ALPHA * resid_ref[...] + acc, g_ref[...], b_ref[...])
    o_ref[...] = y
    ob_ref[...] = y.astype(BF16)


def matmul_ln(a_list, w_list, resid, g, b, *, tm, name):
    m, d = resid.shape
    a_specs, seg_starts, a_args = [], [], []
    for segs in a_list:
        specs, starts, nt = _seg_specs(segs, tm)
        assert nt * tm == m
        a_specs += specs
        seg_starts.append(starts)
        a_args += list(segs)
    in_specs = (a_specs
                + [pl.BlockSpec(w.shape, lambda i: (0, 0)) for w in w_list]
                + [pl.BlockSpec((tm, d), lambda i: (i, 0)),
                   pl.BlockSpec((1, d), lambda i: (0, 0)),
                   pl.BlockSpec((1, d), lambda i: (0, 0))])
    return pl.pallas_call(
        functools.partial(_mm_ln_kernel, seg_starts=tuple(seg_starts)),
        grid=(m // tm,),
        in_specs=in_specs,
        out_specs=[pl.BlockSpec((tm, d), lambda i: (i, 0)),
                   pl.BlockSpec((tm, d), lambda i: (i, 0))],
        out_shape=[jax.ShapeDtypeStruct((m, d), F32), jax.ShapeDtypeStruct((m, d), BF16)],
        compiler_params=_cparams(("parallel",)),
        name=name,
    )(*a_args, *w_list, resid, g.reshape(1, d), b.reshape(1, d))


def _log_sigmoid(z):
    return jnp.minimum(z, 0.0) - jnp.log1p(jnp.exp(-jnp.abs(z)))


def _gla_kernel(*refs, tb, chunk, sub, has_state):
    if has_state:
        q_ref, k_ref, v_ref, og_ref, misc_ref, wa2_ref, ba_ref, gn_ref, s0_ref, o_ref, sout_ref, st_ref = refs
    else:
        q_ref, k_ref, v_ref, og_ref, misc_ref, wa2_ref, ba_ref, gn_ref, o_ref, sout_ref, st_ref = refs
    t = pl.program_id(2)

    @pl.when(t == 0)
    def _():
        if has_state:
            st_ref[...] = s0_ref[...].T
        else:
            st_ref[...] = jnp.zeros_like(st_ref)

    z = jnp.dot(misc_ref[...], wa2_ref[...], preferred_element_type=F32,
                precision=lax.Precision.HIGHEST) + ba_ref[...]
    g_all = _log_sigmoid(z) * (1.0 / GATE_TAU)

    row = lax.broadcasted_iota(jnp.int32, (chunk, chunk), 0)
    col = lax.broadcasted_iota(jnp.int32, (chunk, chunk), 1)
    tril = (row >= col).astype(F32)
    gn = gn_ref[...]

    for c in range(tb // chunk):
        rows = slice(c * chunk, (c + 1) * chunk)
        G = jnp.dot(tril, g_all[rows], preferred_element_type=F32,
                    precision=lax.Precision.HIGHEST)
        qc = q_ref[rows, :] * (GLA_DK ** -0.5)
        kc = k_ref[rows, :]
        vb = v_ref[rows, :].astype(BF16)
        st = st_ref[...]
        o = _dot_nt((qc * jnp.exp(G)).astype(BF16), st.astype(BF16))
        o_parts = []
        for i in range(chunk // sub):
            r0 = i * sub
            nk = r0 + sub
            b_i = G[r0:r0 + 1, :]
            qt = qc[r0:nk] * jnp.exp(G[r0:nk] - b_i)
            kt = kc[:nk] * jnp.exp(b_i - G[:nk])
            a = _dot_nt(qt.astype(BF16), kt.astype(BF16))
            t_idx = r0 + lax.broadcasted_iota(jnp.int32, (sub, nk), 0)
            s_idx = lax.broadcasted_iota(jnp.int32, (sub, nk), 1)
            a = jnp.where(t_idx >= s_idx, a, 0.0)
            o_parts.append(_dot(a.astype(BF16), vb[:nk]))
        o = o + jnp.concatenate(o_parts, axis=0)
        g_last = G[chunk - 1:chunk, :]
        k_dec = kc * jnp.exp(g_last - G)
        st_ref[...] = st * jnp.exp(g_last) + _dot_tn(vb, k_dec.astype(BF16))
        n = o * lax.rsqrt(jnp.mean(o * o, axis=-1, keepdims=True) + LN_EPS) * gn
        og = og_ref[rows, :]
        o_ref[rows, :] = (n * (og * jax.nn.sigmoid(og))).astype(o_ref.dtype)

    @pl.when(t == pl.num_programs(2) - 1)
    def _():
        sout_ref[...] = st_ref[...].T


def gla_mix(pg, wa2p, ba, gn, s0, *, nb, seq, row0, tb, chunk, sub):
    nt = seq // tb
    rb0 = row0 // tb

    def tok(colblock):
        return lambda b, h, t: (rb0 + b * nt + t, colblock(h))

    in_specs = [
        pl.BlockSpec((tb, GLA_DK), tok(lambda h: G_Q // GLA_DK + h)),
        pl.BlockSpec((tb, GLA_DK), tok(lambda h: G_K // GLA_DK + h)),
        pl.BlockSpec((tb, GLA_DV), tok(lambda h: G_V // GLA_DV + h)),
        pl.BlockSpec((tb, GLA_DV), tok(lambda h: G_OG // GLA_DV + h)),
        pl.BlockSpec((tb, LANE), tok(lambda h: G_MISC // LANE)),
        pl.BlockSpec((None, LANE, GLA_DK), lambda b, h, t: (h, 0, 0)),
        pl.BlockSpec((None, 1, GLA_DK), lambda b, h, t: (h, 0, 0)),
        pl.BlockSpec((1, GLA_DV), lambda b, h, t: (0, 0)),
    ]
    args = [pg, pg, pg, pg, pg, wa2p, ba, gn]
    if s0 is not None:
        in_specs.append(pl.BlockSpec((None, None, GLA_DK, GLA_DV), lambda b, h, t: (b, h, 0, 0)))
        args.append(s0)
    return pl.pallas_call(
        functools.partial(_gla_kernel, tb=tb, chunk=chunk, sub=sub, has_state=s0 is not None),
        grid=(nb, GLA_H, nt),
        in_specs=in_specs,
        out_specs=[pl.BlockSpec((tb, GLA_DV), lambda b, h, t: (b * nt + t, h)),
                   pl.BlockSpec((None, None, GLA_DK, GLA_DV), lambda b, h, t: (b, h, 0, 0))],
        out_shape=[jax.ShapeDtypeStruct((nb * seq, GLA_H * GLA_DV), BF16),
                   jax.ShapeDtypeStruct((nb, GLA_H, GLA_DK, GLA_DV), F32)],
        scratch_shapes=[pltpu.VMEM((GLA_DV, GLA_DK), F32)],
        compiler_params=_cparams(("parallel", "parallel", "arbitrary")),
        name="gla",
    )(*args)


INT_MIN = -2 ** 31
NEG_INF = float("-inf")


def _order_key(score):
    bits = pltpu.bitcast(score, jnp.int32)
    return jnp.where(bits < 0, bits ^ jnp.int32(0x7FFFFFFF), bits)


def _count(mask):
    return jnp.sum(jnp.where(mask, 1.0, 0.0), axis=1, keepdims=True)


def _topk_bias(score, adm, kidx, key_ref, topk):
    nq, n = score.shape
    idx_bits = int(np.ceil(np.log2(n + 1)))
    score = jnp.where(adm, score, NEG_INF)
    key_ref[...] = _order_key(score)

    def thr_body(i, t):
        cand = t + lax.shift_left(jnp.int32(1), 31 - i)
        return jnp.where(_count(key_ref[...] >= cand) >= float(topk), cand, t)

    thr = lax.fori_loop(0, 32, thr_body, jnp.full((nq, 1), INT_MIN, jnp.int32))

    key = key_ref[...]
    above = key > thr
    tied = key == thr
    need = float(topk) - _count(above)
    key_neg_inf = jnp.int32(np.array(-np.inf, np.float32).view(np.int32) ^ 0x7FFFFFFF)
    tie_rows = (_count(tied) > need) & (thr > key_neg_inf)
    any_tie = jnp.max(jnp.where(tie_rows, 1.0, 0.0)) > 0.0

    def tie_cut():
        def cut_body(i, j):
            cand = j + lax.shift_left(jnp.int32(1), idx_bits - 1 - i)
            return jnp.where(_count((key_ref[...] == thr) & (kidx < cand)) <= need, cand, j)
        return lax.fori_loop(0, idx_bits, cut_body, jnp.zeros((nq, 1), jnp.int32))

    cut = lax.cond(any_tie, tie_cut, lambda: jnp.full((nq, 1), 2 ** idx_bits, jnp.int32))
    sel = (above | (tied & (kidx < cut))) & adm & (score < float("inf"))
    return jnp.where(sel, 0.0, NEG_INF)


def _idx_weight(misc, h):
    return misc[:, MISC_IW + h:MISC_IW + h + 1] * (IDX_H ** -0.5 * IDX_D ** -0.5)


def _softmax_pv(s, vb):
    m = jnp.max(s, axis=1, keepdims=True)
    p = jnp.exp(s - m)
    l = jnp.sum(p, axis=1, keepdims=True)
    return _dot(p.astype(BF16), vb) / l


def _dsa_prompt_kernel(q_ref, qi_ref, misc_ref, k_ref, v_ref, ki_ref, o_ref, key_ref,
                       *, nq, seq, chunk, topk, step):
    qb = pl.program_id(1)
    lane = lax.broadcasted_iota(jnp.int32, (nq, LANE), 1)
    upper = lane >= IDX_D
    misc = misc_ref[...]
    groups = DSA_H // DSA_KVH

    def body(n):
        kib = ki_ref[pl.ds(0, n), :].astype(BF16)
        score = jnp.zeros((nq, n), F32)
        for h in range(IDX_H):
            pair = qi_ref[:, (h // 2) * LANE:(h // 2 + 1) * LANE]
            qm = jnp.where(upper == (h % 2 == 1), pair, 0.0).astype(BF16)
            score = score + jnp.maximum(_dot_nt(qm, kib), 0.0) * _idx_weight(misc, h)
        kidx = lax.broadcasted_iota(jnp.int32, (nq, n), 1)
        qpos = qb * nq + lax.broadcasted_iota(jnp.int32, (nq, 1), 0)
        limit = (lax.shift_right_logical(qpos, int(np.log2(chunk))) + 1) * chunk
        bias = _topk_bias(score, kidx < limit, kidx, key_ref.at[:, pl.ds(0, n)], topk)
        for j in range(DSA_KVH):
            kb = k_ref[pl.ds(0, n), j * DSA_HD:(j + 1) * DSA_HD].astype(BF16)
            vb = v_ref[pl.ds(0, n), j * DSA_HD:(j + 1) * DSA_HD].astype(BF16)
            for g in range(groups):
                h = j * groups + g
                qh = q_ref[:, h * DSA_HD:(h + 1) * DSA_HD].astype(BF16)
                s = _dot_nt(qh, kb) * (DSA_HD ** -0.5) + bias
                o_ref[:, h * DSA_HD:(h + 1) * DSA_HD] = _softmax_pv(s, vb).astype(o_ref.dtype)

    n_need = (qb + 1) * nq
    cls = (n_need + step - 1) // step - 1
    for c in range(seq // step):
        pl.when(cls == c)(functools.partial(body, (c + 1) * step))


def dsa_prompt(pr, pi, pg, *, nb, seq, nq, chunk, topk, step):
    nqb = seq // nq
    kw = DSA_KVH * DSA_HD
    qmap = lambda cb: (lambda b, t: (b * nqb + t, cb))
    return pl.pallas_call(
        functools.partial(_dsa_prompt_kernel, nq=nq, seq=seq, chunk=chunk, topk=topk, step=step),
        grid=(nb, nqb),
        in_specs=[pl.BlockSpec((nq, DSA_H * DSA_HD), qmap(0)),
                  pl.BlockSpec((nq, IDX_H * IDX_D), qmap(0)),
                  pl.BlockSpec((nq, LANE), qmap(G_MISC // LANE)),
                  pl.BlockSpec((seq, kw), lambda b, t: (b, DSA_H * DSA_HD // kw)),
                  pl.BlockSpec((seq, kw), lambda b, t: (b, G_DV // kw)),
                  pl.BlockSpec((seq, LANE), lambda b, t: (b, IDX_H * IDX_D // LANE))],
        out_specs=pl.BlockSpec((nq, DSA_H * DSA_HD), lambda b, t: (b * nqb + t, 0)),
        out_shape=jax.ShapeDtypeStruct((nb * seq, DSA_H * DSA_HD), BF16),
        scratch_shapes=[pltpu.VMEM((nq, seq), jnp.int32)],
        compiler_params=_cparams(("parallel", "arbitrary")),
        name="dsa_prompt",
    )(pr, pi, pg, pr, pg, pi)


def _dsa_sample_kernel(q_ref, qi_ref, misc_ref, kn_ref, vn_ref, kin_ref, ck_ref, cv_ref, cik_ref, o_ref, key_ref,
                       *, nq, past, topk):
    misc = misc_ref[...]
    groups = DSA_H // DSA_KVH
    zrows = LANE - nq
    cikb = cik_ref[...].astype(BF16)
    kinb = jnp.concatenate([kin_ref[:, :IDX_D], jnp.zeros((zrows, IDX_D), F32)], 0).astype(BF16)
    qs = jnp.concatenate([qi_ref[:, h * IDX_D:(h + 1) * IDX_D] for h in range(IDX_H)], 0).astype(BF16)
    lg_c = _dot_nt(qs, cikb)
    lg_n = _dot_nt(qs, kinb)
    sc_c = jnp.zeros((nq, past), F32)
    sc_n = jnp.zeros((nq, LANE), F32)
    for h in range(IDX_H):
        w = _idx_weight(misc, h)
        sc_c = sc_c + jnp.maximum(lg_c[h * nq:(h + 1) * nq], 0.0) * w
        sc_n = sc_n + jnp.maximum(lg_n[h * nq:(h + 1) * nq], 0.0) * w
    n = past + LANE
    score = jnp.concatenate([sc_c, sc_n], 1)
    kidx = lax.broadcasted_iota(jnp.int32, (nq, n), 1)
    bias = _topk_bias(score, kidx < past + nq, kidx, key_ref, topk)
    bias_g = jnp.concatenate([bias] * groups, 0)
    for j in range(DSA_KVH):
        cols = slice(j * DSA_HD, (j + 1) * DSA_HD)
        kb_c = ck_ref[pl.ds(j, past, stride=DSA_KVH), :].astype(BF16)
        vb_c = cv_ref[pl.ds(j, past, stride=DSA_KVH), :].astype(BF16)
        kb_n = jnp.concatenate([kn_ref[:, cols], jnp.zeros((zrows, DSA_HD), F32)], 0).astype(BF16)
        vb_n = jnp.concatenate([vn_ref[:, cols], jnp.zeros((zrows, DSA_HD), F32)], 0).astype(BF16)
        qg = jnp.concatenate([q_ref[:, (j * groups + g) * DSA_HD:(j * groups + g + 1) * DSA_HD]
                              for g in range(groups)], 0).astype(BF16)
        s = jnp.concatenate([_dot_nt(qg, kb_c), _dot_nt(qg, kb_n)], 1) * (DSA_HD ** -0.5) + bias_g
        m = jnp.max(s, axis=1, keepdims=True)
        p = jnp.exp(s - m)
        l = jnp.sum(p, axis=1, keepdims=True)
        pb = p.astype(BF16)
        o = (_dot(pb[:, :past], vb_c) + _dot(pb[:, past:], vb_n)) / l
        for g in range(groups):
            h = j * groups + g
            o_ref[:, h * DSA_HD:(h + 1) * DSA_HD] = o[g * nq:(g + 1) * nq].astype(o_ref.dtype)


def dsa_sample(pr, pi, pg, cache_k, cache_v, cache_ik, layer, *, nb, nq, row0, past, topk):
    rb0 = row0 // nq
    kw = DSA_KVH * DSA_HD
    smap = lambda cb: (lambda b: (rb0 + b, cb))
    depth = cache_k.shape[0]
    cache_k = cache_k.reshape(depth, nb, past * DSA_KVH, DSA_HD)
    cache_v = cache_v.reshape(depth, nb, past * DSA_KVH, DSA_HD)
    kv_spec = pl.BlockSpec((None, None, past * DSA_KVH, DSA_HD), lambda b: (layer, b, 0, 0))
    return pl.pallas_call(
        functools.partial(_dsa_sample_kernel, nq=nq, past=past, topk=topk),
        grid=(nb,),
        in_specs=[pl.BlockSpec((nq, DSA_H * DSA_HD), smap(0)),
                  pl.BlockSpec((nq, IDX_H * IDX_D), smap(0)),
                  pl.BlockSpec((nq, LANE), smap(G_MISC // LANE)),
                  pl.BlockSpec((nq, kw), smap(DSA_H * DSA_HD // kw)),
                  pl.BlockSpec((nq, kw), smap(G_DV // kw)),
                  pl.BlockSpec((nq, LANE), smap(IDX_H * IDX_D // LANE))]
                 + [kv_spec, kv_spec, pl.BlockSpec((None, None, past, IDX_D), lambda b: (layer, b, 0, 0))],
        out_specs=pl.BlockSpec((nq, DSA_H * DSA_HD), lambda b: (b, 0)),
        out_shape=jax.ShapeDtypeStruct((nb * nq, DSA_H * DSA_HD), BF16),
        scratch_shapes=[pltpu.VMEM((nq, past + LANE), jnp.int32)],
        compiler_params=_cparams(("parallel",)),
        name="dsa_sample",
    )(pr, pi, pg, pr, pg, pi, cache_k, cache_v, cache_ik)


def _xattn_kernel(q_ref, mk_ref, mv_ref, o_ref, *, heads_split):
    for h in range(XA_H):
        cols = slice(h * XA_HD, (h + 1) * XA_HD)
        mk = mk_ref[:, h, :] if heads_split else mk_ref[:, cols]
        mv = mv_ref[:, h, :] if heads_split else mv_ref[:, cols]
        s = _dot_nt(q_ref[:, cols], mk.astype(BF16)) * (XA_HD ** -0.5)
        o_ref[:, cols] = _softmax_pv(s, mv.astype(BF16)).astype(o_ref.dtype)


def cross_attend(q, mk, mv, mem_index, mem_block, *, nb, seq, row0, tq):
    nt = seq // tq
    rb0 = row0 // tq
    d = q.shape[1]
    mem_spec = pl.BlockSpec(mem_block, lambda b, t: mem_index(b))
    return pl.pallas_call(
        functools.partial(_xattn_kernel, heads_split=len([s for s in mem_block if s is not None]) == 3),
        grid=(nb, nt),
        in_specs=[pl.BlockSpec((tq, d), lambda b, t: (rb0 + b * nt + t, 0)), mem_spec, mem_spec],
        out_specs=pl.BlockSpec((tq, d), lambda b, t: (b * nt + t, 0)),
        out_shape=jax.ShapeDtypeStruct((nb * seq, d), BF16),
        compiler_params=_cparams(("parallel", "parallel")),
        name="xattn",
    )(q, mk, mv)


def _top2_sum(vals):
    a, b, c, d = vals
    m1, n1 = jnp.maximum(a, b), jnp.minimum(a, b)
    m2, n2 = jnp.maximum(c, d), jnp.minimum(c, d)
    return jnp.maximum(m1, m2) + jnp.maximum(jnp.minimum(m1, m2), jnp.maximum(n1, n2))


def _router_kernel(x_ref, rwt_ref, rb_ref, comb_ref):
    logits = lax.dot_general(rwt_ref[...], x_ref[...], (((1,), (1,)), ((), ())),
                             preferred_element_type=F32, precision=lax.Precision.HIGHEST)
    aff = jax.nn.sigmoid(logits)
    biased = aff + rb_ref[...]
    a = [aff[e:e + 1, :] for e in range(N_EXPERTS)]
    b = [biased[e:e + 1, :] for e in range(N_EXPERTS)]
    gs = [_top2_sum(b[g * EXP_PER_GROUP:(g + 1) * EXP_PER_GROUP]) for g in range(N_GROUPS)]
    one = jnp.ones_like(a[0])
    zero = jnp.zeros_like(a[0])
    gated = []
    for g in range(N_GROUPS):
        win = one
        for g2 in range(N_GROUPS):
            if g2 < g:
                win = win * jnp.where(gs[g] > gs[g2], one, zero)
            elif g2 > g:
                win = win * jnp.where(gs[g] >= gs[g2], one, zero)
        for e in range(g * EXP_PER_GROUP, (g + 1) * EXP_PER_GROUP):
            beaten = zero
            for f in range(g * EXP_PER_GROUP, (g + 1) * EXP_PER_GROUP):
                if f < e:
                    beaten = beaten + jnp.where(b[f] >= b[e], one, zero)
                elif f > e:
                    beaten = beaten + jnp.where(b[f] > b[e], one, zero)
            gated.append(jnp.where(beaten < 2.0, win, zero) * a[e])
    denom = gated[0]
    for t in gated[1:]:
        denom = denom + t
    comb_ref[...] = jnp.concatenate(gated, axis=0) / denom


def moe_router(x, rwt, rb, *, tm):
    n, d = x.shape
    return pl.pallas_call(
        _router_kernel,
        grid=(n // tm,),
        in_specs=[pl.BlockSpec((tm, d), lambda i: (i, 0)),
                  pl.BlockSpec((N_EXPERTS, d), lambda i: (0, 0)),
                  pl.BlockSpec((N_EXPERTS, 1), lambda i: (0, 0))],
        out_specs=pl.BlockSpec((N_EXPERTS, tm), lambda i: (0, i)),
        out_shape=jax.ShapeDtypeStruct((N_EXPERTS, n), F32),
        compiler_params=_cparams(("parallel",)),
        name="router",
    )(x, rwt, rb)


def _moe_kernel(xb_ref, x_ref, comb_ref, wg_ref, wu_ref, wd_ref, g_ref, b_ref, o_ref, ob_ref, acc_ref):
    e = pl.program_id(1)

    @pl.when(e == 0)
    def _():
        acc_ref[...] = jnp.zeros_like(acc_ref)

    xb = xb_ref[...]
    hg = _dot(xb, wg_ref[...])
    hu = _dot(xb, wu_ref[...])
    comb = comb_ref[...]
    lane = lax.broadcasted_iota(jnp.int32, comb.shape, 1)
    c = jnp.sum(jnp.where(lane == e, comb, 0.0), axis=1, keepdims=True)
    hid = (hg * jax.nn.sigmoid(hg)) * hu * c
    acc_ref[...] += _dot(hid.astype(BF16), wd_ref[...])

    @pl.when(e == pl.num_programs(1) - 1)
    def _():
        y = _layer_norm_rows(DN_ALPHA * x_ref[...] + acc_ref[...], g_ref[...], b_ref[...])
        o_ref[...] = y
        ob_ref[...] = y.astype(BF16)


def moe_ffn_ln(xb, x, comb, wg, wu, wd, g, b, *, tm):
    n, d = x.shape
    return pl.pallas_call(
        _moe_kernel,
        grid=(n // tm, N_EXPERTS),
        in_specs=[pl.BlockSpec((tm, d), lambda i, e: (i, 0)),
                  pl.BlockSpec((tm, d), lambda i, e: (i, 0)),
                  pl.BlockSpec((tm, N_EXPERTS), lambda i, e: (i, 0)),
                  pl.BlockSpec((None, d, D_FF), lambda i, e: (e, 0, 0)),
                  pl.BlockSpec((None, d, D_FF), lambda i, e: (e, 0, 0)),
                  pl.BlockSpec((None, D_FF, d), lambda i, e: (e, 0, 0)),
                  pl.BlockSpec((1, d), lambda i, e: (0, 0)),
                  pl.BlockSpec((1, d), lambda i, e: (0, 0))],
        out_specs=[pl.BlockSpec((tm, d), lambda i, e: (i, 0)),
                   pl.BlockSpec((tm, d), lambda i, e: (i, 0))],
        out_shape=[jax.ShapeDtypeStruct((n, d), F32), jax.ShapeDtypeStruct((n, d), BF16)],
        scratch_shapes=[pltpu.VMEM((tm, d), F32)],
        compiler_params=_cparams(("parallel", "arbitrary")),
        name="moe",
    )(xb, x, comb, wg, wu, wd, g.reshape(1, d), b.reshape(1, d))


def _rope_tables(pos, head_dim):
    half = head_dim // 2
    inv_freq = ROPE_THETA ** (-jnp.arange(half, dtype=F32) / half)
    ang = pos.astype(F32)[:, None] * inv_freq[None, :]
    cos, sin = jnp.cos(ang), jnp.sin(ang)
    reps = LANE // head_dim
    return (jnp.tile(jnp.concatenate([cos, cos], -1), (1, reps)),
            jnp.tile(jnp.concatenate([-sin, sin], -1), (1, reps)))


def _pack_w_in(w):
    gq, gk, gv, glr, gog, dq, dk, dv, iq, ik, iw = jnp.split(w, SPLIT_POINTS, axis=-1)
    pad = jnp.zeros((w.shape[0], LANE - GATE_RANK - IDX_H), w.dtype)
    wg = jnp.concatenate([gq, gk, gv, gog, dv, glr, iw, pad], -1).astype(BF16)
    wr = jnp.concatenate([dq, dk], -1).astype(BF16)
    wi = jnp.concatenate([iq, ik, ik], -1).astype(BF16)
    return wg, wr, wi


def _pack_wa2(wa2):
    w = wa2.reshape(GATE_RANK, GLA_H, GLA_DK).transpose(1, 0, 2)
    return jnp.pad(w, ((0, 0), (MISC_LR, LANE - GATE_RANK - MISC_LR), (0, 0)))


def kernel(x_prompt, x_sample, cache_dsa_k, cache_dsa_v, cache_idx_k, state_gla, cache_mem_k, cache_mem_v,
           mem_prompt, ln_in_g, ln_in_b, w_in, gla_wa2, gla_ba, gla_norm_g, w_out, ln1_g, ln1_b,
           xa_wq, xa_wk, xa_wv, xa_wo, ln2_g, ln2_b, router_w, router_b,
           moe_w_gate, moe_w_up, moe_w_down, ln3_g, ln3_b):
    tm = 768
    tm_ln = 256
    x, xb = layer_norm_rows([x_prompt.reshape(N_PROMPT, D_MODEL), x_sample.reshape(N_SAMPLE, D_MODEL)],
                            ln_in_g, ln_in_b, tm_ln)

    pos = jnp.concatenate([jnp.tile(jnp.arange(SEQ), BATCH),
                           jnp.tile(PAST_LEN + jnp.arange(DEC_SEQ), DEC_BATCH)])
    rope128 = _rope_tables(pos, DSA_HD) + (DSA_HD // 2,)
    rope64 = _rope_tables(pos, IDX_D) + (IDX_D // 2,)
    memb = mem_prompt.reshape(BATCH * N_MEM, D_MODEL).astype(BF16)
    rwt = router_w.T
    rb = router_b.reshape(N_EXPERTS, 1)

    p_states, s_states = [], []
    for l in range(DEPTH):
        wg, wr, wi = _pack_w_in(w_in[l])
        pg = matmul(xb, wg, tm=tm, tn=1152, name="in_proj_g")
        pr = matmul(xb, wr, tm=tm, tn=WR_COLS, rope=rope128, name="in_proj_r")
        pi = matmul(xb, wi, tm=tm, tn=WI_COLS, rope=rope64, name="in_proj_i")

        wa2p = _pack_wa2(gla_wa2[l])
        ba = gla_ba[l].reshape(GLA_H, 1, GLA_DK)
        gn = gla_norm_g[l].reshape(1, GLA_DV)
        mixg_p, st_p = gla_mix(pg, wa2p, ba, gn, None, nb=BATCH, seq=SEQ, row0=0,
                               tb=256, chunk=CHUNK, sub=GLA_SUB)
        mixg_s, st_s = gla_mix(pg, wa2p, ba, gn, state_gla[l], nb=DEC_BATCH, seq=DEC_SEQ, row0=N_PROMPT,
                               tb=DEC_SEQ, chunk=DEC_SEQ, sub=DEC_SEQ)

        od_p = dsa_prompt(pr, pi, pg, nb=BATCH, seq=SEQ, nq=128, chunk=CHUNK,
                          topk=min(TOPK_MAX, SEQ // 4), step=512)
        od_s = dsa_sample(pr, pi, pg, cache_dsa_k, cache_dsa_v, cache_idx_k, l, nb=DEC_BATCH, nq=DEC_SEQ,
                          row0=N_PROMPT, past=PAST_LEN, topk=min(TOPK_MAX, (PAST_LEN + DEC_SEQ) // 4))

        wo = w_out[l].astype(BF16)
        x1, x1b = matmul_ln([[mixg_p, mixg_s], [od_p, od_s]], [wo[:GLA_H * GLA_DV], wo[GLA_H * GLA_DV:]],
                            x, ln1_g[l], ln1_b[l], tm=tm_ln, name="out_proj_ln1")

        q_xa = matmul(x1b, xa_wq[l].astype(BF16), tm=tm, tn=1024, out_dtype=BF16, name="xa_q")
        mk_p = matmul(memb, xa_wk[l].astype(BF16), tm=512, tn=1024, name="mem_k")
        mv_p = matmul(memb, xa_wv[l].astype(BF16), tm=512, tn=1024, name="mem_v")
        xa_p = cross_attend(q_xa, mk_p, mv_p, lambda b: (b, 0), (N_MEM, D_MODEL),
                            nb=BATCH, seq=SEQ, row0=0, tq=512)
        xa_s = cross_attend(q_xa, cache_mem_k, cache_mem_v, lambda b, l=l: (l, b, 0, 0, 0),
                            (None, None, N_MEM, XA_H, XA_HD), nb=DEC_BATCH, seq=DEC_SEQ, row0=N_PROMPT, tq=DEC_SEQ)
        x2, x2b = matmul_ln([[xa_p, xa_s]], [xa_wo[l].astype(BF16)], x1, ln2_g[l], ln2_b[l],
                            tm=tm_ln, name="xa_o_ln2")

        comb = moe_router(x2, rwt, rb, tm=tm).T
        x, xb = moe_ffn_ln(x2b, x2, comb, moe_w_gate[l].astype(BF16), moe_w_up[l].astype(BF16),
                           moe_w_down[l].astype(BF16), ln3_g[l], ln3_b[l], tm=384)

        kv = DSA_KVH * DSA_HD
        p_states.append((pr[:N_PROMPT, DSA_H * DSA_HD:].reshape(BATCH, SEQ, DSA_KVH, DSA_HD),
                         pg[:N_PROMPT, G_DV:G_DV + kv].reshape(BATCH, SEQ, DSA_KVH, DSA_HD),
                         pi[:N_PROMPT, IDX_H * IDX_D:IDX_H * IDX_D + IDX_D].reshape(BATCH, SEQ, IDX_D),
                         st_p,
                         mk_p.reshape(BATCH, N_MEM, XA_H, XA_HD),
                         mv_p.reshape(BATCH, N_MEM, XA_H, XA_HD)))
        s_states.append((pr[N_PROMPT:, DSA_H * DSA_HD:].reshape(DEC_BATCH, DEC_SEQ, DSA_KVH, DSA_HD),
                         pg[N_PROMPT:, G_DV:G_DV + kv].reshape(DEC_BATCH, DEC_SEQ, DSA_KVH, DSA_HD),
                         pi[N_PROMPT:, IDX_H * IDX_D:IDX_H * IDX_D + IDX_D].reshape(DEC_BATCH, DEC_SEQ, IDX_D),
                         st_s))

    def stack(states, i):
        return jnp.stack([st[i] for st in states])

    return (x[:N_PROMPT].reshape(BATCH, SEQ, D_MODEL), x[N_PROMPT:].reshape(DEC_BATCH, DEC_SEQ, D_MODEL),
            stack(p_states, 0), stack(p_states, 1), stack(p_states, 2), stack(p_states, 3),
            stack(p_states, 4), stack(p_states, 5),
            stack(s_states, 0), stack(s_states, 1), stack(s_states, 2), stack(s_states, 3))
```

```python
import functools

import jax
import jax.numpy as jnp
import numpy as np
from jax import lax
from jax.experimental import pallas as pl
from jax.experimental.pallas import tpu as pltpu

F32 = jnp.float32
BF16 = jnp.bfloat16

D_MODEL = 2048
BATCH = 4
SEQ = 2048
DEPTH = 2
DEC_BATCH = 16
DEC_SEQ = 16
PAST_LEN = 4096
CHUNK = 64
N_MEM = 256
GLA_H = 4
GLA_DK = 128
GLA_DV = 256
GATE_RANK = 16
GATE_TAU = 16.0
DSA_H = 8
DSA_KVH = 2
DSA_HD = 128
IDX_H = 16
IDX_D = 64
TOPK_MAX = 256
XA_H = 4
XA_HD = D_MODEL // XA_H
N_EXPERTS = 16
N_GROUPS = 4
EXP_PER_GROUP = N_EXPERTS // N_GROUPS
D_FF = 512
ROPE_THETA = 10000.0
LN_EPS = 1e-5
DN_ALPHA = (2 * DEPTH) ** 0.25

IN_SPLITS = (GLA_H * GLA_DK, GLA_H * GLA_DK, GLA_H * GLA_DV, GATE_RANK, GLA_H * GLA_DV,
             DSA_H * DSA_HD, DSA_KVH * DSA_HD, DSA_KVH * DSA_HD, IDX_H * IDX_D, IDX_D, IDX_H)
SPLIT_POINTS = tuple(int(c) for c in np.cumsum(IN_SPLITS)[:-1])

N_PROMPT = BATCH * SEQ
N_SAMPLE = DEC_BATCH * DEC_SEQ
N_TOK = N_PROMPT + N_SAMPLE

LANE = 128
GLA_SUB = 16
VMEM_LIMIT = 48 * 1024 * 1024

G_Q, G_K, G_V, G_OG, G_DV, G_MISC = 0, 512, 1024, 2048, 3072, 3328
WG_COLS = 3456
WR_COLS = 1280
WI_COLS = 1152
MISC_LR = 0
MISC_IW = GATE_RANK


def _cparams(sem):
    return pltpu.CompilerParams(dimension_semantics=sem, vmem_limit_bytes=VMEM_LIMIT)


def _dot(a, b):
    return jnp.dot(a, b, preferred_element_type=F32)


def _dot_nt(a, b):
    return lax.dot_general(a, b, (((1,), (1,)), ((), ())), preferred_element_type=F32)


def _dot_tn(a, b):
    return lax.dot_general(a, b, (((0,), (0,)), ((), ())), preferred_element_type=F32)


def _layer_norm_rows(y, g, b):
    mu = jnp.mean(y, axis=-1, keepdims=True)
    d = y - mu
    var = jnp.mean(d * d, axis=-1, keepdims=True)
    return d * lax.rsqrt(var + LN_EPS) * g + b


def _seg_specs(segs, tm):
    specs, starts, start = [], [], 0
    for a in segs:
        nt = a.shape[0] // tm
        specs.append(pl.BlockSpec((tm, a.shape[1]),
                                  functools.partial(lambda i, s0, n: (jnp.clip(i - s0, 0, n - 1), 0), s0=start, n=nt)))
        starts.append(start)
        start += nt
    return specs, tuple(starts), start


def _seg_load(refs, starts):
    i = pl.program_id(0)
    v = refs[0][...]
    for r, s0 in zip(refs[1:], starts[1:]):
        v = jnp.where(i >= s0, r[...], v)
    return v


def _ln_kernel(*refs, starts):
    n = len(starts)
    g_ref, b_ref, o_ref, ob_ref = refs[n:]
    y = _layer_norm_rows(_seg_load(refs[:n], starts), g_ref[...], b_ref[...])
    o_ref[...] = y
    ob_ref[...] = y.astype(BF16)


def layer_norm_rows(segs, g, b, tm):
    d = segs[0].shape[1]
    specs, starts, nt = _seg_specs(segs, tm)
    return pl.pallas_call(
        functools.partial(_ln_kernel, starts=starts),
        grid=(nt,),
        in_specs=specs + [pl.BlockSpec((1, d), lambda i: (0, 0)),
                          pl.BlockSpec((1, d), lambda i: (0, 0))],
        out_specs=[pl.BlockSpec((tm, d), lambda i: (i, 0)),
                   pl.BlockSpec((tm, d), lambda i: (i, 0))],
        out_shape=[jax.ShapeDtypeStruct((nt * tm, d), F32), jax.ShapeDtypeStruct((nt * tm, d), BF16)],
        compiler_params=_cparams(("parallel",)),
        name="ln_in",
    )(*segs, g.reshape(1, d), b.reshape(1, d))


def _swap_halves(x, half):
    if half == LANE // 2:
        return pltpu.roll(x, half, 1)
    lane = lax.broadcasted_iota(jnp.int32, x.shape, 1)
    lower = (lane % (2 * half)) < half
    return jnp.where(lower, pltpu.roll(x, LANE - half, 1), pltpu.roll(x, half, 1))


def _mm_kernel(a_ref, w_ref, *rest, rope_half, tn):
    acc = _dot(a_ref[...], w_ref[...])
    if rope_half is None:
        (o_ref,) = rest
        o_ref[...] = acc.astype(o_ref.dtype)
        return
    cos_ref, sin_ref, o_ref = rest
    cos = cos_ref[...]
    sin = sin_ref[...]
    for c in range(tn // LANE):
        x = acc[:, c * LANE:(c + 1) * LANE]
        o_ref[:, c * LANE:(c + 1) * LANE] = (x * cos + _swap_halves(x, rope_half) * sin).astype(o_ref.dtype)


def matmul(a, w, *, tm, tn, out_dtype=F32, rope=None, name="mm"):
    m, k = a.shape
    n = w.shape[1]
    in_specs = [pl.BlockSpec((tm, k), lambda j, i: (i, 0)),
                pl.BlockSpec((k, tn), lambda j, i: (0, j))]
    args = [a, w]
    rope_half = None
    if rope is not None:
        cos, sin, rope_half = rope
        in_specs += [pl.BlockSpec((tm, LANE), lambda j, i: (i, 0)),
                     pl.BlockSpec((tm, LANE), lambda j, i: (i, 0))]
        args += [cos, sin]
    return pl.pallas_call(
        functools.partial(_mm_kernel, rope_half=rope_half, tn=tn),
        grid=(n // tn, m // tm),
        in_specs=in_specs,
        out_specs=pl.BlockSpec((tm, tn), lambda j, i: (i, j)),
        out_shape=jax.ShapeDtypeStruct((m, n), out_dtype),
        compiler_params=_cparams(("parallel", "parallel")),
        name=name,
    )(*args)


def _mm_ln_kernel(*refs, seg_starts):
    pos = 0
    a_vals = []
    for starts in seg_starts:
        a_vals.append(_seg_load(refs[pos:pos + len(starts)], starts))
        pos += len(starts)
    n_pairs = len(seg_starts)
    w_refs = refs[pos:pos + n_pairs]
    resid_ref, g_ref, b_ref, o_ref, ob_ref = refs[pos + n_pairs:]
    acc = _dot(a_vals[0], w_refs[0][...])
    for a, w_ref in zip(a_vals[1:], w_refs[1:]):
        acc = acc + _dot(a, w_ref[...])
    y = _layer_norm_rows(DN_ALPHA * resid_ref[...] + acc, g_ref[...], b_ref[...])
    o_ref[...] = y
    ob_ref[...] = y.astype(BF16)


def matmul_ln(a_list, w_list, resid, g, b, *, tm, name):
    m, d = resid.shape
    a_specs, seg_starts, a_args = [], [], []
    for segs in a_list:
        specs, starts, nt = _seg_specs(segs, tm)
        assert nt * tm == m
        a_specs += specs
        seg_starts.append(starts)
        a_args += list(segs)
    in_specs = (a_specs
                + [pl.BlockSpec(w.shape, lambda i: (0, 0)) for w in w_list]
                + [pl.BlockSpec((tm, d), lambda i: (i, 0)),
                   pl.BlockSpec((1, d), lambda i: (0, 0)),
                   pl.BlockSpec((1, d), lambda i: (0, 0))])
    return pl.pallas_call(
        functools.partial(_mm_ln_kernel, seg_starts=tuple(seg_starts)),
        grid=(m // tm,),
        in_specs=in_specs,
        out_specs=[pl.BlockSpec((tm, d), lambda i: (i, 0)),
                   pl.BlockSpec((tm, d), lambda i: (i, 0))],
        out_shape=[jax.ShapeDtypeStruct((m, d), F32), jax.ShapeDtypeStruct((m, d), BF16)],
        compiler_params=_cparams(("parallel",)),
        name=name,
    )(*a_args, *w_list, resid, g.reshape(1, d), b.reshape(1, d))


def _log_sigmoid(z):
    return jnp.minimum(z, 0.0) - jnp.log1p(jnp.exp(-jnp.abs(z)))


def _gla_kernel(*refs, tb, chunk, sub, has_state):
    if has_state:
        q_ref, k_ref, v_ref, og_ref, misc_ref, wa2_ref, ba_ref, gn_ref, s0_ref, o_ref, sout_ref, st_ref = refs
    else:
        q_ref, k_ref, v_ref, og_ref, misc_ref, wa2_ref, ba_ref, gn_ref, o_ref, sout_ref, st_ref = refs
    t = pl.program_id(2)

    @pl.when(t == 0)
    def _():
        if has_state:
            st_ref[...] = s0_ref[...].T
        else:
            st_ref[...] = jnp.zeros_like(st_ref)

    z = jnp.dot(misc_ref[...], wa2_ref[...], preferred_element_type=F32,
                precision=lax.Precision.HIGHEST) + ba_ref[...]
    g_all = _log_sigmoid(z) * (1.0 / GATE_TAU)

    row = lax.broadcasted_iota(jnp.int32, (chunk, chunk), 0)
    col = lax.broadcasted_iota(jnp.int32, (chunk, chunk), 1)
    tril = (row >= col).astype(F32)
    gn = gn_ref[...]

    for c in range(tb // chunk):
        rows = slice(c * chunk, (c + 1) * chunk)
        G = jnp.dot(tril, g_all[rows], preferred_element_type=F32,
                    precision=lax.Precision.HIGHEST)
        qc = q_ref[rows, :] * (GLA_DK ** -0.5)
        kc = k_ref[rows, :]
        vb = v_ref[rows, :].astype(BF16)
        st = st_ref[...]
        o = _dot_nt((qc * jnp.exp(G)).astype(BF16), st.astype(BF16))
        o_parts = []
        for i in range(chunk // sub):
            r0 = i * sub
            nk = r0 + sub
            b_i = G[r0:r0 + 1, :]
            qt = qc[r0:nk] * jnp.exp(G[r0:nk] - b_i)
            kt = kc[:nk] * jnp.exp(b_i - G[:nk])
            a = _dot_nt(qt.astype(BF16), kt.astype(BF16))
            t_idx = r0 + lax.broadcasted_iota(jnp.int32, (sub, nk), 0)
            s_idx = lax.broadcasted_iota(jnp.int32, (sub, nk), 1)
            a = jnp.where(t_idx >= s_idx, a, 0.0)
            o_parts.append(_dot(a.astype(BF16), vb[:nk]))
        o = o + jnp.concatenate(o_parts, axis=0)
        g_last = G[chunk - 1:chunk, :]
        k_dec = kc * jnp.exp(g_last - G)
        st_ref[...] = st * jnp.exp(g_last) + _dot_tn(vb, k_dec.astype(BF16))
        n = o * lax.rsqrt(jnp.mean(o * o, axis=-1, keepdims=True) + LN_EPS) * gn
        og = og_ref[rows, :]
        o_ref[rows, :] = (n * (og * jax.nn.sigmoid(og))).astype(o_ref.dtype)

    @pl.when(t == pl.num_programs(2) - 1)
    def _():
        sout_ref[...] = st_ref[...].T


def gla_mix(pg, wa2p, ba, gn, s0, *, nb, seq, row0, tb, chunk, sub):
    nt = seq // tb
    rb0 = row0 // tb

    def tok(colblock):
        return lambda b, h, t: (rb0 + b * nt + t, colblock(h))

    in_specs = [
        pl.BlockSpec((tb, GLA_DK), tok(lambda h: G_Q // GLA_DK + h)),
        pl.BlockSpec((tb, GLA_DK), tok(lambda h: G_K // GLA_DK + h)),
        pl.BlockSpec((tb, GLA_DV), tok(lambda h: G_V // GLA_DV + h)),
        pl.BlockSpec((tb, GLA_DV), tok(lambda h: G_OG // GLA_DV + h)),
        pl.BlockSpec((tb, LANE), tok(lambda h: G_MISC // LANE)),
        pl.BlockSpec((None, LANE, GLA_DK), lambda b, h, t: (h, 0, 0)),
        pl.BlockSpec((None, 1, GLA_DK), lambda b, h, t: (h, 0, 0)),
        pl.BlockSpec((1, GLA_DV), lambda b, h, t: (0, 0)),
    ]
    args = [pg, pg, pg, pg, pg, wa2p, ba, gn]
    if s0 is not None:
        in_specs.append(pl.BlockSpec((None, None, GLA_DK, GLA_DV), lambda b, h, t: (b, h, 0, 0)))
        args.append(s0)
    return pl.pallas_call(
        functools.partial(_gla_kernel, tb=tb, chunk=chunk, sub=sub, has_state=s0 is not None),
        grid=(nb, GLA_H, nt),
        in_specs=in_specs,
        out_specs=[pl.BlockSpec((tb, GLA_DV), lambda b, h, t: (b * nt + t, h)),
                   pl.BlockSpec((None, None, GLA_DK, GLA_DV), lambda b, h, t: (b, h, 0, 0))],
        out_shape=[jax.ShapeDtypeStruct((nb * seq, GLA_H * GLA_DV), BF16),
                   jax.ShapeDtypeStruct((nb, GLA_H, GLA_DK, GLA_DV), F32)],
        scratch_shapes=[pltpu.VMEM((GLA_DV, GLA_DK), F32)],
        compiler_params=_cparams(("parallel", "parallel", "arbitrary")),
        name="gla",
    )(*args)


INT_MIN = -2 ** 31
NEG_INF = float("-inf")


def _order_key(score):
    bits = pltpu.bitcast(score, jnp.int32)
    return jnp.where(bits < 0, bits ^ jnp.int32(0x7FFFFFFF), bits)


def _count(mask):
    return jnp.sum(jnp.where(mask, 1.0, 0.0), axis=1, keepdims=True)


def _topk_bias(score, adm, kidx, key_ref, topk):
    nq, n = score.shape
    idx_bits = int(np.ceil(np.log2(n + 1)))
    score = jnp.where(adm, score, NEG_INF)
    key_ref[...] = _order_key(score)

    def thr_body(i, t):
        cand = t + lax.shift_left(jnp.int32(1), 31 - i)
        return jnp.where(_count(key_ref[...] >= cand) >= float(topk), cand, t)

    thr = lax.fori_loop(0, 32, thr_body, jnp.full((nq, 1), INT_MIN, jnp.int32))

    key = key_ref[...]
    above = key > thr
    tied = key == thr
    need = float(topk) - _count(above)
    key_neg_inf = jnp.int32(np.array(-np.inf, np.float32).view(np.int32) ^ 0x7FFFFFFF)
    tie_rows = (_count(tied) > need) & (thr > key_neg_inf)
    any_tie = jnp.max(jnp.where(tie_rows, 1.0, 0.0)) > 0.0

    def tie_cut():
        def cut_body(i, j):
            cand = j + lax.shift_left(jnp.int32(1), idx_bits - 1 - i)
            return jnp.where(_count((key_ref[...] == thr) & (kidx < cand)) <= need, cand, j)
        return lax.fori_loop(0, idx_bits, cut_body, jnp.zeros((nq, 1), jnp.int32))

    cut = lax.cond(any_tie, tie_cut, lambda: jnp.full((nq, 1), 2 ** idx_bits, jnp.int32))
    sel = (above | (tied & (kidx < cut))) & adm & (score < float("inf"))
    return jnp.where(sel, 0.0, NEG_INF)


def _idx_weight(misc, h):
    return misc[:, MISC_IW + h:MISC_IW + h + 1] * (IDX_H ** -0.5 * IDX_D ** -0.5)


def _softmax_pv(s, vb):
    m = jnp.max(s, axis=1, keepdims=True)
    p = jnp.exp(s - m)
    l = jnp.sum(p, axis=1, keepdims=True)
    return _dot(p.astype(BF16), vb) / l


def _dsa_prompt_kernel(q_ref, qi_ref, misc_ref, k_ref, v_ref, ki_ref, o_ref, key_ref,
                       *, nq, seq, chunk, topk, step):
    qb = pl.program_id(1)
    lane = lax.broadcasted_iota(jnp.int32, (nq, LANE), 1)
    upper = lane >= IDX_D
    misc = misc_ref[...]
    groups = DSA_H // DSA_KVH

    def body(n):
        kib = ki_ref[pl.ds(0, n), :].astype(BF16)
        score = jnp.zeros((nq, n), F32)
        for h in range(IDX_H):
            pair = qi_ref[:, (h // 2) * LANE:(h // 2 + 1) * LANE]
            qm = jnp.where(upper == (h % 2 == 1), pair, 0.0).astype(BF16)
            score = score + jnp.maximum(_dot_nt(qm, kib), 0.0) * _idx_weight(misc, h)
        kidx = lax.broadcasted_iota(jnp.int32, (nq, n), 1)
        qpos = qb * nq + lax.broadcasted_iota(jnp.int32, (nq, 1), 0)
        limit = (lax.shift_right_logical(qpos, int(np.log2(chunk))) + 1) * chunk
        bias = _topk_bias(score, kidx < limit, kidx, key_ref.at[:, pl.ds(0, n)], topk)
        for j in range(DSA_KVH):
            kb = k_ref[pl.ds(0, n), j * DSA_HD:(j + 1) * DSA_HD].astype(BF16)
            vb = v_ref[pl.ds(0, n), j * DSA_HD:(j + 1) * DSA_HD].astype(BF16)
            for g in range(groups):
                h = j * groups + g
                qh = q_ref[:, h * DSA_HD:(h + 1) * DSA_HD].astype(BF16)
                s = _dot_nt(qh, kb) * (DSA_HD ** -0.5) + bias
                o_ref[:, h * DSA_HD:(h + 1) * DSA_HD] = _softmax_pv(s, vb).astype(o_ref.dtype)

    n_need = (qb + 1) * nq
    cls = (n_need + step - 1) // step - 1
    for c in range(seq // step):
        pl.when(cls == c)(functools.partial(body, (c + 1) * step))


def dsa_prompt(pr, pi, pg, *, nb, seq, nq, chunk, topk, step):
    nqb = seq // nq
    kw = DSA_KVH * DSA_HD
    qmap = lambda cb: (lambda b, t: (b * nqb + t, cb))
    return pl.pallas_call(
        functools.partial(_dsa_prompt_kernel, nq=nq, seq=seq, chunk=chunk, topk=topk, step=step),
        grid=(nb, nqb),
        in_specs=[pl.BlockSpec((nq, DSA_H * DSA_HD), qmap(0)),
                  pl.BlockSpec((nq, IDX_H * IDX_D), qmap(0)),
                  pl.BlockSpec((nq, LANE), qmap(G_MISC // LANE)),
                  pl.BlockSpec((seq, kw), lambda b, t: (b, DSA_H * DSA_HD // kw)),
                  pl.BlockSpec((seq, kw), lambda b, t: (b, G_DV // kw)),
                  pl.BlockSpec((seq, LANE), lambda b, t: (b, IDX_H * IDX_D // LANE))],
        out_specs=pl.BlockSpec((nq, DSA_H * DSA_HD), lambda b, t: (b * nqb + t, 0)),
        out_shape=jax.ShapeDtypeStruct((nb * seq, DSA_H * DSA_HD), BF16),
        scratch_shapes=[pltpu.VMEM((nq, seq), jnp.int32)],
        compiler_params=_cparams(("parallel", "arbitrary")),
        name="dsa_prompt",
    )(pr, pi, pg, pr, pg, pi)


def _dsa_sample_kernel(q_ref, qi_ref, misc_ref, kn_ref, vn_ref, kin_ref, ck_ref, cv_ref, cik_ref, o_ref, key_ref,
                       *, nq, past, topk):
    misc = misc_ref[...]
    groups = DSA_H // DSA_KVH
    zrows = LANE - nq
    cikb = cik_ref[...].astype(BF16)
    kinb = jnp.concatenate([kin_ref[:, :IDX_D], jnp.zeros((zrows, IDX_D), F32)], 0).astype(BF16)
    qs = jnp.concatenate([qi_ref[:, h * IDX_D:(h + 1) * IDX_D] for h in range(IDX_H)], 0).astype(BF16)
    lg_c = _dot_nt(qs, cikb)
    lg_n = _dot_nt(qs, kinb)
    sc_c = jnp.zeros((nq, past), F32)
    sc_n = jnp.zeros((nq, LANE), F32)
    for h in range(IDX_H):
        w = _idx_weight(misc, h)
        sc_c = sc_c + jnp.maximum(lg_c[h * nq:(h + 1) * nq], 0.0) * w
        sc_n = sc_n + jnp.maximum(lg_n[h * nq:(h + 1) * nq], 0.0) * w
    n = past + LANE
    score = jnp.concatenate([sc_c, sc_n], 1)
    kidx = lax.broadcasted_iota(jnp.int32, (nq, n), 1)
    bias = _topk_bias(score, kidx < past + nq, kidx, key_ref, topk)
    bias_g = jnp.concatenate([bias] * groups, 0)
    for j in range(DSA_KVH):
        cols = slice(j * DSA_HD, (j + 1) * DSA_HD)
        kb_c = ck_ref[pl.ds(j, past, stride=DSA_KVH), :].astype(BF16)
        vb_c = cv_ref[pl.ds(j, past, stride=DSA_KVH), :].astype(BF16)
        kb_n = jnp.concatenate([kn_ref[:, cols], jnp.zeros((zrows, DSA_HD), F32)], 0).astype(BF16)
        vb_n = jnp.concatenate([vn_ref[:, cols], jnp.zeros((zrows, DSA_HD), F32)], 0).astype(BF16)
        qg = jnp.concatenate([q_ref[:, (j * groups + g) * DSA_HD:(j * groups + g + 1) * DSA_HD]
                              for g in range(groups)], 0).astype(BF16)
        s = jnp.concatenate([_dot_nt(qg, kb_c), _dot_nt(qg, kb_n)], 1) * (DSA_HD ** -0.5) + bias_g
        m = jnp.max(s, axis=1, keepdims=True)
        p = jnp.exp(s - m)
        l = jnp.sum(p, axis=1, keepdims=True)
        pb = p.astype(BF16)
        o = (_dot(pb[:, :past], vb_c) + _dot(pb[:, past:], vb_n)) / l
        for g in range(groups):
            h = j * groups + g
            o_ref[:, h * DSA_HD:(h + 1) * DSA_HD] = o[g * nq:(g + 1) * nq].astype(o_ref.dtype)


def dsa_sample(pr, pi, pg, cache_k, cache_v, cache_ik, layer, *, nb, nq, row0, past, topk):
    rb0 = row0 // nq
    kw = DSA_KVH * DSA_HD
    smap = lambda cb: (lambda b: (rb0 + b, cb))
    depth = cache_k.shape[0]
    cache_k = cache_k.reshape(depth, nb, past * DSA_KVH, DSA_HD)
    cache_v = cache_v.reshape(depth, nb, past * DSA_KVH, DSA_HD)
    kv_spec = pl.BlockSpec((None, None, past * DSA_KVH, DSA_HD), lambda b: (layer, b, 0, 0))
    return pl.pallas_call(
        functools.partial(_dsa_sample_kernel, nq=nq, past=past, topk=topk),
        grid=(nb,),
        in_specs=[pl.BlockSpec((nq, DSA_H * DSA_HD), smap(0)),
                  pl.BlockSpec((nq, IDX_H * IDX_D), smap(0)),
                  pl.BlockSpec((nq, LANE), smap(G_MISC // LANE)),
                  pl.BlockSpec((nq, kw), smap(DSA_H * DSA_HD // kw)),
                  pl.BlockSpec((nq, kw), smap(G_DV // kw)),
                  pl.BlockSpec((nq, LANE), smap(IDX_H * IDX_D // LANE))]
                 + [kv_spec, kv_spec, pl.BlockSpec((None, None, past, IDX_D), lambda b: (layer, b, 0, 0))],
        out_specs=pl.BlockSpec((nq, DSA_H * DSA_HD), lambda b: (b, 0)),
        out_shape=jax.ShapeDtypeStruct((nb * nq, DSA_H * DSA_HD), BF16),
        scratch_shapes=[pltpu.VMEM((nq, past + LANE), jnp.int32)],
        compiler_params=_cparams(("parallel",)),
        name="dsa_sample",
    )(pr, pi, pg, pr, pg, pi, cache_k, cache_v, cache_ik)


def _xattn_kernel(q_ref, mk_ref, mv_ref, o_ref, *, heads_split):
    for h in range(XA_H):
        cols = slice(h * XA_HD, (h + 1) * XA_HD)
        mk = mk_ref[:, h, :] if heads_split else mk_ref[:, cols]
        mv = mv_ref[:, h, :] if heads_split else mv_ref[:, cols]
        s = _dot_nt(q_ref[:, cols], mk.astype(BF16)) * (XA_HD ** -0.5)
        o_ref[:, cols] = _softmax_pv(s, mv.astype(BF16)).astype(o_ref.dtype)


def cross_attend(q, mk, mv, mem_index, mem_block, *, nb, seq, row0, tq):
    nt = seq // tq
    rb0 = row0 // tq
    d = q.shape[1]
    mem_spec = pl.BlockSpec(mem_block, lambda b, t: mem_index(b))
    return pl.pallas_call(
        functools.partial(_xattn_kernel, heads_split=len([s for s in mem_block if s is not None]) == 3),
        grid=(nb, nt),
        in_specs=[pl.BlockSpec((tq, d), lambda b, t: (rb0 + b * nt + t, 0)), mem_spec, mem_spec],
        out_specs=pl.BlockSpec((tq, d), lambda b, t: (b * nt + t, 0)),
        out_shape=jax.ShapeDtypeStruct((nb * seq, d), BF16),
        compiler_params=_cparams(("parallel", "parallel")),
        name="xattn",
    )(q, mk, mv)


def _top2_sum(vals):
    a, b, c, d = vals
    m1, n1 = jnp.maximum(a, b), jnp.minimum(a, b)
    m2, n2 = jnp.maximum(c, d), jnp.minimum(c, d)
    return jnp.maximum(m1, m2) + jnp.maximum(jnp.minimum(m1, m2), jnp.maximum(n1, n2))


def _router_kernel(x_ref, rwt_ref, rb_ref, comb_ref):
    logits = lax.dot_general(rwt_ref[...], x_ref[...], (((1,), (1,)), ((), ())),
                             preferred_element_type=F32, precision=lax.Precision.HIGHEST)
    aff = jax.nn.sigmoid(logits)
    biased = aff + rb_ref[...]
    a = [aff[e:e + 1, :] for e in range(N_EXPERTS)]
    b = [biased[e:e + 1, :] for e in range(N_EXPERTS)]
    gs = [_top2_sum(b[g * EXP_PER_GROUP:(g + 1) * EXP_PER_GROUP]) for g in range(N_GROUPS)]
    one = jnp.ones_like(a[0])
    zero = jnp.zeros_like(a[0])
    gated = []
    for g in range(N_GROUPS):
        win = one
        for g2 in range(N_GROUPS):
            if g2 < g:
                win = win * jnp.where(gs[g] > gs[g2], one, zero)
            elif g2 > g:
                win = win * jnp.where(gs[g] >= gs[g2], one, zero)
        for e in range(g * EXP_PER_GROUP, (g + 1) * EXP_PER_GROUP):
            beaten = zero
            for f in range(g * EXP_PER_GROUP, (g + 1) * EXP_PER_GROUP):
                if f < e:
                    beaten = beaten + jnp.where(b[f] >= b[e], one, zero)
                elif f > e:
                    beaten = beaten + jnp.where(b[f] > b[e], one, zero)
            gated.append(jnp.where(beaten < 2.0, win, zero) * a[e])
    denom = gated[0]
    for t in gated[1:]:
        denom = denom + t
    comb_ref[...] = jnp.concatenate(gated, axis=0) / denom


def moe_router(x, rwt, rb, *, tm):
    n, d = x.shape
    return pl.pallas_call(
        _router_kernel,
        grid=(n // tm,),
        in_specs=[pl.BlockSpec((tm, d), lambda i: (i, 0)),
                  pl.BlockSpec((N_EXPERTS, d), lambda i: (0, 0)),
                  pl.BlockSpec((N_EXPERTS, 1), lambda i: (0, 0))],
        out_specs=pl.BlockSpec((N_EXPERTS, tm), lambda i: (0, i)),
        out_shape=jax.ShapeDtypeStruct((N_EXPERTS, n), F32),
        compiler_params=_cparams(("parallel",)),
        name="router",
    )(x, rwt, rb)


def _moe_kernel(xb_ref, x_ref, comb_ref, wg_ref, wu_ref, wd_ref, g_ref, b_ref, o_ref, ob_ref, acc_ref):
    e = pl.program_id(1)

    @pl.when(e == 0)
    def _():
        acc_ref[...] = jnp.zeros_like(acc_ref)

    xb = xb_ref[...]
    hg = _dot(xb, wg_ref[...])
    hu = _dot(xb, wu_ref[...])
    comb = comb_ref[...]
    lane = lax.broadcasted_iota(jnp.int32, comb.shape, 1)
    c = jnp.sum(jnp.where(lane == e, comb, 0.0), axis=1, keepdims=True)
    hid = (hg * jax.nn.sigmoid(hg)) * hu * c
    acc_ref[...] += _dot(hid.astype(BF16), wd_ref[...])

    @pl.when(e == pl.num_programs(1) - 1)
    def _():
        y = _layer_norm_rows(DN_ALPHA * x_ref[...] + acc_ref[...], g_ref[...], b_ref[...])
        o_ref[...] = y
        ob_ref[...] = y.astype(BF16)


def moe_ffn_ln(xb, x, comb, wg, wu, wd, g, b, *, tm):
    n, d = x.shape
    return pl.pallas_call(
        _moe_kernel,
        grid=(n // tm, N_EXPERTS),
        in_specs=[pl.BlockSpec((tm, d), lambda i, e: (i, 0)),
                  pl.BlockSpec((tm, d), lambda i, e: (i, 0)),
                  pl.BlockSpec((tm, N_EXPERTS), lambda i, e: (i, 0)),
                  pl.BlockSpec((None, d, D_FF), lambda i, e: (e, 0, 0)),
                  pl.BlockSpec((None, d, D_FF), lambda i, e: (e, 0, 0)),
                  pl.BlockSpec((None, D_FF, d), lambda i, e: (e, 0, 0)),
                  pl.BlockSpec((1, d), lambda i, e: (0, 0)),
                  pl.BlockSpec((1, d), lambda i, e: (0, 0))],
        out_specs=[pl.BlockSpec((tm, d), lambda i, e: (i, 0)),
                   pl.BlockSpec((tm, d), lambda i, e: (i, 0))],
        out_shape=[jax.ShapeDtypeStruct((n, d), F32), jax.ShapeDtypeStruct((n, d), BF16)],
        scratch_shapes=[pltpu.VMEM((tm, d), F32)],
        compiler_params=_cparams(("parallel", "arbitrary")),
        name="moe",
    )(xb, x, comb, wg, wu, wd, g.reshape(1, d), b.reshape(1, d))


def _rope_tables(pos, head_dim):
    half = head_dim // 2
    inv_freq = ROPE_THETA ** (-jnp.arange(half, dtype=F32) / half)
    ang = pos.astype(F32)[:, None] * inv_freq[None, :]
    cos, sin = jnp.cos(ang), jnp.sin(ang)
    reps = LANE // head_dim
    return (jnp.tile(jnp.concatenate([cos, cos], -1), (1, reps)),
            jnp.tile(jnp.concatenate([-sin, sin], -1), (1, reps)))


def _pack_w_in(w):
    gq, gk, gv, glr, gog, dq, dk, dv, iq, ik, iw = jnp.split(w, SPLIT_POINTS, axis=-1)
    pad = jnp.zeros((w.shape[0], LANE - GATE_RANK - IDX_H), w.dtype)
    wg = jnp.concatenate([gq, gk, gv, gog, dv, glr, iw, pad], -1).astype(BF16)
    wr = jnp.concatenate([dq, dk], -1).astype(BF16)
    wi = jnp.concatenate([iq, ik, ik], -1).astype(BF16)
    return wg, wr, wi


def _pack_wa2(wa2):
    w = wa2.reshape(GATE_RANK, GLA_H, GLA_DK).transpose(1, 0, 2)
    return jnp.pad(w, ((0, 0), (MISC_LR, LANE - GATE_RANK - MISC_LR), (0, 0)))


def kernel(x_prompt, x_sample, cache_dsa_k, cache_dsa_v, cache_idx_k, state_gla, cache_mem_k, cache_mem_v,
           mem_prompt, ln_in_g, ln_in_b, w_in, gla_wa2, gla_ba, gla_norm_g, w_out, ln1_g, ln1_b,
           xa_wq, xa_wk, xa_wv, xa_wo, ln2_g, ln2_b, router_w, router_b,
           moe_w_gate, moe_w_up, moe_w_down, ln3_g, ln3_b):
    tm = 768
    tm_ln = 256
    x, xb = layer_norm_rows([x_prompt.reshape(N_PROMPT, D_MODEL), x_sample.reshape(N_SAMPLE, D_MODEL)],
                            ln_in_g, ln_in_b, tm_ln)

    pos = jnp.concatenate([jnp.tile(jnp.arange(SEQ), BATCH),
                           jnp.tile(PAST_LEN + jnp.arange(DEC_SEQ), DEC_BATCH)])
    rope128 = _rope_tables(pos, DSA_HD) + (DSA_HD // 2,)
    rope64 = _rope_tables(pos, IDX_D) + (IDX_D // 2,)
    memb = mem_prompt.reshape(BATCH * N_MEM, D_MODEL).astype(BF16)
    rwt = router_w.T
    rb = router_b.reshape(N_EXPERTS, 1)

    p_states, s_states = [], []
    for l in range(DEPTH):
        wg, wr, wi = _pack_w_in(w_in[l])
        pg = matmul(xb, wg, tm=tm, tn=1152, name="in_proj_g")
        pr = matmul(xb, wr, tm=tm, tn=WR_COLS, rope=rope128, name="in_proj_r")
        pi = matmul(xb, wi, tm=tm, tn=WI_COLS, rope=rope64, name="in_proj_i")

        wa2p = _pack_wa2(gla_wa2[l])
        ba = gla_ba[l].reshape(GLA_H, 1, GLA_DK)
        gn = gla_norm_g[l].reshape(1, GLA_DV)
        mixg_p, st_p = gla_mix(pg, wa2p, ba, gn, None, nb=BATCH, seq=SEQ, row0=0,
                               tb=512, chunk=CHUNK, sub=GLA_SUB)
        mixg_s, st_s = gla_mix(pg, wa2p, ba, gn, state_gla[l], nb=DEC_BATCH, seq=DEC_SEQ, row0=N_PROMPT,
                               tb=DEC_SEQ, chunk=DEC_SEQ, sub=DEC_SEQ)

        od_p = dsa_prompt(pr, pi, pg, nb=BATCH, seq=SEQ, nq=128, chunk=CHUNK,
                          topk=min(TOPK_MAX, SEQ // 4), step=512)
        od_s = dsa_sample(pr, pi, pg, cache_dsa_k, cache_dsa_v, cache_idx_k, l, nb=DEC_BATCH, nq=DEC_SEQ,
                          row0=N_PROMPT, past=PAST_LEN, topk=min(TOPK_MAX, (PAST_LEN + DEC_SEQ) // 4))

        wo = w_out[l].astype(BF16)
        x1, x1b = matmul_ln([[mixg_p, mixg_s], [od_p, od_s]], [wo[:GLA_H * GLA_DV], wo[GLA_H * GLA_DV:]],
                            x, ln1_g[l], ln1_b[l], tm=tm_ln, name="out_proj_ln1")

        q_xa = matmul(x1b, xa_wq[l].astype(BF16), tm=tm, tn=1024, out_dtype=BF16, name="xa_q")
        mk_p = matmul(memb, xa_wk[l].astype(BF16), tm=512, tn=1024, name="mem_k")
        mv_p = matmul(memb, xa_wv[l].astype(BF16), tm=512, tn=1024, name="mem_v")
        xa_p = cross_attend(q_xa, mk_p, mv_p, lambda b: (b, 0), (N_MEM, D_MODEL),
                            nb=BATCH, seq=SEQ, row0=0, tq=512)
        xa_s = cross_attend(q_xa, cache_mem_k, cache_mem_v, lambda b, l=l: (l, b, 0, 0, 0),
                            (None, None, N_MEM, XA_H, XA_HD), nb=DEC_BATCH, seq=DEC_SEQ, row0=N_PROMPT, tq=DEC_SEQ)
        x2, x2b = matmul_ln([[xa_p, xa_s]], [xa_wo[l].astype(BF16)], x1, ln2_g[l], ln2_b[l],
                            tm=tm_ln, name="xa_o_ln2")

        comb = moe_router(x2, rwt, rb, tm=tm).T
        x, xb = moe_ffn_ln(x2b, x2, comb, moe_w_gate[l].astype(BF16), moe_w_up[l].astype(BF16),
                           moe_w_down[l].astype(BF16), ln3_g[l], ln3_b[l], tm=384)

        kv = DSA_KVH * DSA_HD
        p_states.append((pr[:N_PROMPT, DSA_H * DSA_HD:].reshape(BATCH, SEQ, DSA_KVH, DSA_HD),
                         pg[:N_PROMPT, G_DV:G_DV + kv].reshape(BATCH, SEQ, DSA_KVH, DSA_HD),
                         pi[:N_PROMPT, IDX_H * IDX_D:IDX_H * IDX_D + IDX_D].reshape(BATCH, SEQ, IDX_D),
                         st_p,
                         mk_p.reshape(BATCH, N_MEM, XA_H, XA_HD),
                         mv_p.reshape(BATCH, N_MEM, XA_H, XA_HD)))
        s_states.append((pr[N_PROMPT:, DSA_H * DSA_HD:].reshape(DEC_BATCH, DEC_SEQ, DSA_KVH, DSA_HD),
                         pg[N_PROMPT:, G_DV:G_DV + kv].reshape(DEC_BATCH, DEC_SEQ, DSA_KVH, DSA_HD),
                         pi[N_PROMPT:, IDX_H * IDX_D:IDX_H * IDX_D + IDX_D].reshape(DEC_BATCH, DEC_SEQ, IDX_D),
                         st_s))

    def stack(states, i):
        return jnp.stack([st[i] for st in states])

    return (x[:N_PROMPT].reshape(BATCH, SEQ, D_MODEL), x[N_PROMPT:].reshape(DEC_BATCH, DEC_SEQ, D_MODEL),
            stack(p_states, 0), stack(p_states, 1), stack(p_states, 2), stack(p_states, 3),
            stack(p_states, 4), stack(p_states, 5),
            stack(s_states, 0), stack(s_states, 1), stack(s_states, 2), stack(s_states, 3))
```
